```python
import math
import jax, jax.numpy as jnp
from jax import lax
import numpy as np

D_MODEL = 1024
BATCH = 8
SEQ = 2048
DEPTH = 4

CONV_DIM = D_MODEL
CONV_KERNEL = 31
RET_QK_DIM = 256
RET_V_DIM = 512
RET_HEADS = D_MODEL // RET_QK_DIM
RET_QK_WIDTH = RET_HEADS * RET_QK_DIM
RET_V_WIDTH = RET_HEADS * RET_V_DIM
RET_CHUNK = 128
ROPE_BASE = 10000.0
N_BRANCHES = 2
IN_SIZES = (CONV_DIM, CONV_DIM, RET_QK_WIDTH, RET_QK_WIDTH, RET_V_WIDTH, RET_V_WIDTH, D_MODEL, D_MODEL)
IN_WIDTH = sum(IN_SIZES)
IN_SPLITS = tuple(int(s) for s in np.cumsum(IN_SIZES)[:-1])
D_FF = 2816
N_EXPERTS = 8
TOP_K = 2
N_DENSE = (DEPTH + 1) // 2
N_MOE = DEPTH // 2
DEEPNORM_ALPHA = (2 * DEPTH) ** 0.25
DEEPNORM_BETA = (8 * DEPTH) ** -0.25
N_ADA = 6
LN_EPS = 1e-5

kernel_name = "hybrid_conv_retention_moe_deepnorm_adaln"


def layer_norm(x, g, b):
    xf = x.astype(jnp.float32)
    mu = jnp.mean(xf, axis=-1, keepdims=True)
    var = jnp.mean(jnp.square(xf - mu), axis=-1, keepdims=True)
    y = (xf - mu) * lax.rsqrt(var + LN_EPS)
    return (y * g.astype(jnp.float32) + b.astype(jnp.float32)).astype(x.dtype)


def head_norm(x):
    mu = jnp.mean(x, axis=-1, keepdims=True)
    var = jnp.mean(jnp.square(x - mu), axis=-1, keepdims=True)
    return (x - mu) * lax.rsqrt(var + LN_EPS)


def rotary(x, positions):
    half = x.shape[-1] // 2
    inv_freq = ROPE_BASE ** (-jnp.arange(half, dtype=jnp.float32) / half)
    ang = positions.astype(jnp.float32)[:, :, None] * inv_freq
    cos = jnp.cos(ang)[:, :, None, :]
    sin = jnp.sin(ang)[:, :, None, :]
    xf = x.astype(jnp.float32)
    x1, x2 = xf[..., :half], xf[..., half:]
    return jnp.concatenate([x1 * cos - x2 * sin, x2 * cos + x1 * sin], axis=-1)


def chunkwise_retention(q, k, v):
    b, t, h, dk = q.shape
    dv = v.shape[-1]
    n_chunks = t // RET_CHUNK
    log_g = jnp.log1p(-jnp.exp2(-5.0 - jnp.arange(h, dtype=jnp.float32)))
    j = jnp.arange(RET_CHUNK, dtype=jnp.float32)
    diff = j[:, None] - j[None, :]
    intra = jnp.where(diff >= 0, jnp.exp(log_g[:, None, None] * jnp.maximum(diff, 0.0)), 0.0)
    q_decay = jnp.exp(log_g[:, None] * (j + 1.0))[None, :, :, None]
    k_decay = jnp.exp(log_g[:, None] * (RET_CHUNK - 1.0 - j))[None, :, :, None]
    chunk_decay = jnp.exp(log_g * RET_CHUNK)[None, :, None, None]

    def to_chunks(a):
        return a.reshape(b, n_chunks, RET_CHUNK, h, a.shape[-1]).transpose(1, 0, 3, 2, 4)

    qc, kc, vc = to_chunks(q), to_chunks(k * (dk ** -0.5)), to_chunks(v)

    def step(state, inp):
        qi, ki, vi = inp
        scores = jnp.einsum('bhid,bhjd->bhij', qi, ki) * intra
        inner = jnp.einsum('bhij,bhjv->bhiv', scores, vi)
        cross = jnp.einsum('bhid,bhdv->bhiv', qi * q_decay, state)
        new_state = state * chunk_decay + jnp.einsum('bhjd,bhjv->bhdv', ki * k_decay, vi)
        return new_state, inner + cross

    state0 = jnp.zeros((b, h, dk, dv), jnp.float32)
    _, out = lax.scan(step, state0, (qc, kc, vc))
    return out.transpose(1, 0, 3, 2, 4).reshape(b, t, h, dv)


def causal_depthwise_conv(u, w, bias):
    y = lax.conv_general_dilated(
        u, w[:, None, :].astype(u.dtype), window_strides=(1,), padding=[(CONV_KERNEL - 1, 0)],
        dimension_numbers=('NWC', 'WIO', 'NWC'), feature_group_count=u.shape[-1])
    return y + bias


def token_mixer(h, positions, w_in, w_dw, b_dw, ln_conv_g, ln_conv_b, w_conv_o, w_ret_o, w_out):
    b, t, _ = h.shape
    proj = h @ w_in
    conv_a, conv_b, q, k, v, ret_gate, gate_conv, gate_ret = jnp.split(proj, IN_SPLITS, axis=-1)
    u = conv_a * jax.nn.sigmoid(conv_b)
    u = causal_depthwise_conv(u, w_dw, b_dw)
    u = jax.nn.silu(layer_norm(u, ln_conv_g, ln_conv_b))
    y_conv = u @ w_conv_o
    q = rotary(q.reshape(b, t, RET_HEADS, RET_QK_DIM), positions)
    k = rotary(k.reshape(b, t, RET_HEADS, RET_QK_DIM), positions)
    v = v.reshape(b, t, RET_HEADS, RET_V_DIM).astype(jnp.float32)
    o = head_norm(chunkwise_retention(q, k, v)).reshape(b, t, RET_V_WIDTH).astype(h.dtype)
    y_ret = (jax.nn.silu(ret_gate) * o) @ w_ret_o
    merged = jax.nn.sigmoid(gate_conv) * y_conv + jax.nn.sigmoid(gate_ret) * y_ret
    return merged @ w_out


def swiglu(h, w_gate, w_up, w_down):
    return (jax.nn.silu(h @ w_gate) * (h @ w_up)) @ w_down


def moe_swiglu(h, w_router, w_gate, w_up, w_down):
    b, t, d = h.shape
    tokens = h.reshape(b * t, d)
    logits = (tokens @ w_router).astype(jnp.float32)
    top_vals, top_idx = lax.top_k(logits, TOP_K)
    top_w = jax.nn.softmax(top_vals, axis=-1)
    combine = jnp.sum(jax.nn.one_hot(top_idx, N_EXPERTS, dtype=jnp.float32) * top_w[..., None], axis=1)
    combine = combine.astype(tokens.dtype)
    out = jnp.zeros_like(tokens)
    for e in range(N_EXPERTS):
        out = out + combine[:, e:e + 1] * swiglu(tokens, w_gate[e], w_up[e], w_down[e])
    return out.reshape(b, t, d)


def setup_inputs(seed: int = 0) -> dict:
    key = jax.random.key(seed)
    ks = jax.random.split(key, 32)
    f32 = jnp.float32
    D = D_MODEL

    def nrm(k, shape, scale):
        return jax.random.normal(k, shape, f32) * scale

    x = jax.random.normal(ks[0], (BATCH, SEQ, D), f32)
    c = jax.random.normal(ks[1], (BATCH, D), f32)
    positions = (jnp.arange(SEQ, dtype=jnp.int32)[None, :]
                 + jax.random.randint(ks[2], (BATCH, 1), 0, 1024, dtype=jnp.int32))
    return {
        "x": x,
        "c": c,
        "positions": positions,
        "w_ada": nrm(ks[3], (DEPTH, D, N_ADA * D), D ** -0.5),
        "b_ada": nrm(ks[4], (DEPTH, N_ADA * D), 0.02),
        "w_in": nrm(ks[5], (DEPTH, D, IN_WIDTH), D ** -0.5),
        "w_dw": nrm(ks[6], (DEPTH, CONV_KERNEL, CONV_DIM), CONV_KERNEL ** -0.5),
        "b_dw": nrm(ks[7], (DEPTH, CONV_DIM), 0.02),
        "ln_conv_g": 1.0 + nrm(ks[8], (DEPTH, CONV_DIM), 0.02),
        "ln_conv_b": nrm(ks[9], (DEPTH, CONV_DIM), 0.02),
        "w_conv_o": nrm(ks[10], (DEPTH, CONV_DIM, D), CONV_DIM ** -0.5),
        "w_ret_o": nrm(ks[11], (DEPTH, RET_V_WIDTH, D), RET_V_WIDTH ** -0.5),
        "w_out": nrm(ks[12], (DEPTH, D, D), DEEPNORM_BETA * D ** -0.5),
        "ln1_g": 1.0 + nrm(ks[13], (DEPTH, D), 0.02),
        "ln1_b": nrm(ks[14], (DEPTH, D), 0.02),
        "ffn_w_gate": nrm(ks[15], (N_DENSE, D, D_FF), D ** -0.5),
        "ffn_w_up": nrm(ks[16], (N_DENSE, D, D_FF), D ** -0.5),
        "ffn_w_down": nrm(ks[17], (N_DENSE, D_FF, D), DEEPNORM_BETA * D_FF ** -0.5),
        "moe_w_router": nrm(ks[18], (N_MOE, D, N_EXPERTS), D ** -0.5),
        "moe_w_gate": nrm(ks[19], (N_MOE, N_EXPERTS, D, D_FF), D ** -0.5),
        "moe_w_up": nrm(ks[20], (N_MOE, N_EXPERTS, D, D_FF), D ** -0.5),
        "moe_w_down": nrm(ks[21], (N_MOE, N_EXPERTS, D_FF, D), DEEPNORM_BETA * D_FF ** -0.5),
        "ln2_g": 1.0 + nrm(ks[22], (DEPTH, D), 0.02),
        "ln2_b": nrm(ks[23], (DEPTH, D), 0.02),
    }


def reference(x, c, positions, w_ada, b_ada, w_in, w_dw, b_dw, ln_conv_g, ln_conv_b,
              w_conv_o, w_ret_o, w_out, ln1_g, ln1_b, ffn_w_gate, ffn_w_up, ffn_w_down,
              moe_w_router, moe_w_gate, moe_w_up, moe_w_down, ln2_g, ln2_b):
    c_act = jax.nn.silu(c)
    for l in range(DEPTH):
        ada = c_act @ w_ada[l] + b_ada[l]
        shift1, scale1, gate1, shift2, scale2, gate2 = [a[:, None, :] for a in jnp.split(ada, N_ADA, axis=-1)]
        h = x * (1.0 + scale1) + shift1
        mix = token_mixer(h, positions, w_in[l], w_dw[l], b_dw[l], ln_conv_g[l], ln_conv_b[l],
                          w_conv_o[l], w_ret_o[l], w_out[l])
        x = layer_norm(DEEPNORM_ALPHA * x + gate1 * mix, ln1_g[l], ln1_b[l])
        h = x * (1.0 + scale2) + shift2
        if l % 2 == 0:
            i = l // 2
            ff = swiglu(h, ffn_w_gate[i], ffn_w_up[i], ffn_w_down[i])
        else:
            i = l // 2
            ff = moe_swiglu(h, moe_w_router[i], moe_w_gate[i], moe_w_up[i], moe_w_down[i])
        x = layer_norm(DEEPNORM_ALPHA * x + gate2 * ff, ln2_g[l], ln2_b[l])
    return x
```

```python
import functools

import jax
import jax.numpy as jnp
import numpy as np
from jax import lax
from jax.experimental import pallas as pl
from jax.experimental.pallas import tpu as pltpu

D_MODEL = 1024
DEPTH = 4
CONV_KERNEL = 31
RET_QK_DIM = 256
RET_V_DIM = 512
RET_HEADS = 4
RET_V_WIDTH = RET_HEADS * RET_V_DIM
RET_CHUNK = 128
ROPE_BASE = 10000.0
D_FF = 2816
N_EXPERTS = 8
N_ADA = 6
DEEPNORM_ALPHA = (2 * DEPTH) ** 0.25
LN_EPS = 1e-5

LANES = 128
CONV_HALO = 32
CONV_ROWS = 16
VMEM_LIMIT = 56 * 1024 * 1024

F32 = jnp.float32
BF16 = jnp.bfloat16


def _params(*sem):
    return pltpu.CompilerParams(dimension_semantics=sem, vmem_limit_bytes=VMEM_LIMIT)


def _bdot(a, b):
    return jnp.dot(a, b, preferred_element_type=F32)


def _split_bf16(a):
    hi = a.astype(BF16)
    lo = (a - hi.astype(F32)).astype(BF16)
    return hi, lo


def _dot3(a, b):
    a_hi, a_lo = _split_bf16(a)
    b_hi, b_lo = _split_bf16(b)
    return _bdot(a_hi, b_hi) + (_bdot(a_hi, b_lo) + _bdot(a_lo, b_hi))


def _sigmoid(x):
    return jax.nn.sigmoid(x)


def _layer_norm_rows(y, g, b):
    mu = jnp.mean(y, axis=-1, keepdims=True)
    d = y - mu
    var = jnp.mean(d * d, axis=-1, keepdims=True)
    return d * lax.rsqrt(var + LN_EPS) * g + b


def _ada_kernel(c_ref, w_ref, b_ref, o_ref):
    c = c_ref[...]
    o_ref[0] = _dot3(c * _sigmoid(c), w_ref[0]) + b_ref[0]


def _ada_call(c, w_ada, b_ada):
    depth, d, width = w_ada.shape
    bsz = c.shape[0]
    ncol = width // d
    out = pl.pallas_call(
        _ada_kernel,
        grid=(depth, ncol),
        in_specs=[
            pl.BlockSpec((bsz, d), lambda l, j: (0, 0)),
            pl.BlockSpec((1, d, d), lambda l, j: (l, 0, j)),
            pl.BlockSpec((1, 1, d), lambda l, j: (l, 0, j)),
        ],
        out_specs=pl.BlockSpec((1, bsz, d), lambda l, j: (l, 0, j)),
        out_shape=jax.ShapeDtypeStruct((depth, bsz, width), F32),
        compiler_params=_params("parallel", "parallel"),
        name="ada",
    )(c, w_ada, b_ada.reshape(depth, 1, width))
    return out.reshape(depth, bsz, N_ADA, d)


def _rope_kernel(pos_ref, inv_ref, cos_ref, sin_ref):
    ang = pos_ref[...].astype(F32) * inv_ref[...]
    cos_ref[...] = jnp.cos(ang)
    sin_ref[...] = jnp.sin(ang)


def _rope_call(positions, tm):
    n = positions.size
    half = RET_QK_DIM // 2
    inv_freq = ROPE_BASE ** (-jnp.arange(half, dtype=F32) / half)
    return pl.pallas_call(
        _rope_kernel,
        grid=(n // tm,),
        in_specs=[
            pl.BlockSpec((tm, 1), lambda i: (i, 0)),
            pl.BlockSpec((1, half), lambda i: (0, 0)),
        ],
        out_specs=[pl.BlockSpec((tm, half), lambda i: (i, 0))] * 2,
        out_shape=[jax.ShapeDtypeStruct((n, half), F32)] * 2,
        compiler_params=_params("parallel"),
        name="rope",
    )(positions.reshape(n, 1), inv_freq.reshape(1, half))


def _modulate_kernel(x_ref, ada_ref, h_ref):
    shift = ada_ref[0, 0:1, :]
    scale = ada_ref[0, 1:2, :]
    h_ref[...] = (x_ref[...] * (1.0 + scale) + shift).astype(BF16)


def _modulate_call(xf, ada_l, seq, tm):
    n, d = xf.shape
    per = seq // tm
    return pl.pallas_call(
        _modulate_kernel,
        grid=(n // tm,),
        in_specs=[
            pl.BlockSpec((tm, d), lambda i: (i, 0)),
            pl.BlockSpec((1, N_ADA, d), lambda i: (i // per, 0, 0)),
        ],
        out_specs=pl.BlockSpec((tm, d), lambda i: (i, 0)),
        out_shape=jax.ShapeDtypeStruct((n, d), BF16),
        compiler_params=_params("parallel"),
        name="modulate",
    )(xf, ada_l)


def _glu_kernel(h_ref, wa_ref, wb_ref, u_ref):
    h = h_ref[...]
    a = _bdot(h, wa_ref[...])
    b = _bdot(h, wb_ref[...])
    u_ref[...] = a * _sigmoid(b)


def _glu_call(h, w_in_b, tm):
    n, d = h.shape
    return pl.pallas_call(
        _glu_kernel,
        grid=(n // tm,),
        in_specs=[
            pl.BlockSpec((tm, d), lambda i: (i, 0)),
            pl.BlockSpec((d, d), lambda i: (0, 0)),
            pl.BlockSpec((d, d), lambda i: (0, 1)),
        ],
        out_specs=pl.BlockSpec((tm, d), lambda i: (i, 0)),
        out_shape=jax.ShapeDtypeStruct((n, d), F32),
        compiler_params=_params("parallel"),
        name="proj_glu",
    )(h, w_in_b, w_in_b)


def _qk_kernel(h_ref, w_ref, cos_ref, sin_ref, o_ref):
    acc = _bdot(h_ref[...], w_ref[...])
    scale = jnp.where(pl.program_id(0) == 0, 1.0, RET_QK_DIM ** -0.5).astype(F32)
    cos = cos_ref[...] * scale
    sin = sin_ref[...] * scale
    half = RET_QK_DIM // 2
    for hd in range(RET_HEADS):
        lo = hd * RET_QK_DIM
        x1 = acc[:, lo:lo + half]
        x2 = acc[:, lo + half:lo + RET_QK_DIM]
        o_ref[0, :, lo:lo + half] = x1 * cos - x2 * sin
        o_ref[0, :, lo + half:lo + RET_QK_DIM] = x2 * cos + x1 * sin


def _qk_call(h, w_in_b, cos, sin, tm):
    n, d = h.shape
    half = RET_QK_DIM // 2
    return pl.pallas_call(
        _qk_kernel,
        grid=(2, n // tm),
        in_specs=[
            pl.BlockSpec((tm, d), lambda j, i: (i, 0)),
            pl.BlockSpec((d, d), lambda j, i: (0, 2 + j)),
            pl.BlockSpec((tm, half), lambda j, i: (i, 0)),
            pl.BlockSpec((tm, half), lambda j, i: (i, 0)),
        ],
        out_specs=pl.BlockSpec((1, tm, d), lambda j, i: (j, i, 0)),
        out_shape=jax.ShapeDtypeStruct((2, n, d), F32),
        compiler_params=_params("parallel", "parallel"),
        name="proj_qk",
    )(h, w_in_b, cos, sin)


def _act_kernel(h_ref, w_ref, o_ref, *, act):
    acc = _bdot(h_ref[...], w_ref[...])
    if act == "silu":
        acc = acc * _sigmoid(acc)
    elif act == "sigmoid":
        acc = _sigmoid(acc)
    o_ref[...] = acc.astype(o_ref.dtype)


def _act_call(h, w_in_b, col0, ncols, act, dtype, tm, name):
    n, d = h.shape
    return pl.pallas_call(
        functools.partial(_act_kernel, act=act),
        grid=(ncols, n // tm),
        in_specs=[
            pl.BlockSpec((tm, d), lambda j, i: (i, 0)),
            pl.BlockSpec((d, d), lambda j, i: (0, col0 + j)),
        ],
        out_specs=pl.BlockSpec((tm, d), lambda j, i: (i, j)),
        out_shape=jax.ShapeDtypeStruct((n, ncols * d), dtype),
        compiler_params=_params("parallel", "parallel"),
        name=name,
    )(h, w_in_b)


def _conv_kernel(cur_ref, prev_ref, w_ref, bias_ref, g_ref, b_ref, o_ref, buf_ref, *, tt):
    first = pl.program_id(1) == 0
    buf_ref[0:CONV_HALO, :] = jnp.where(first, 0.0, prev_ref[0])
    buf_ref[CONV_HALO:, :] = cur_ref[0]
    bias = bias_ref[...]
    g = g_ref[...]
    b = b_ref[...]
    base = CONV_HALO - (CONV_KERNEL - 1)
    for r0 in range(0, tt, CONV_ROWS):
        acc = jnp.zeros((CONV_ROWS, cur_ref.shape[-1]), F32)
        for k in range(CONV_KERNEL):
            acc = acc + buf_ref[r0 + base + k:r0 + base + k + CONV_ROWS, :] * w_ref[k:k + 1, :]
        y = _layer_norm_rows(acc + bias, g, b)
        o_ref[0, r0:r0 + CONV_ROWS, :] = (y * _sigmoid(y)).astype(BF16)


def _conv_call(u, w_dw, b_dw, ln_g, ln_b, tt):
    bsz, seq, ch = u.shape
    per = tt // CONV_HALO
    return pl.pallas_call(
        functools.partial(_conv_kernel, tt=tt),
        grid=(bsz, seq // tt),
        in_specs=[
            pl.BlockSpec((1, tt, ch), lambda b, t: (b, t, 0)),
            pl.BlockSpec((1, CONV_HALO, ch), lambda b, t: (b, jnp.maximum(t * per - 1, 0), 0)),
            pl.BlockSpec((CONV_KERNEL, ch), lambda b, t: (0, 0)),
            pl.BlockSpec((1, ch), lambda b, t: (0, 0)),
            pl.BlockSpec((1, ch), lambda b, t: (0, 0)),
            pl.BlockSpec((1, ch), lambda b, t: (0, 0)),
        ],
        out_specs=pl.BlockSpec((1, tt, ch), lambda b, t: (b, t, 0)),
        out_shape=jax.ShapeDtypeStruct((bsz, seq, ch), BF16),
        scratch_shapes=[pltpu.VMEM((CONV_HALO + tt, ch), F32)],
        compiler_params=_params("parallel", "parallel"),
        name="conv_ln",
    )(u, u, w_dw, b_dw.reshape(1, ch), ln_g.reshape(1, ch), ln_b.reshape(1, ch))


def _ret_kernel(q_ref, k_ref, v_ref, g_ref, o_ref, s_ref, *, chunk):
    hd = pl.program_id(1)

    @pl.when(pl.program_id(2) == 0)
    def _():
        s_ref[...] = jnp.zeros_like(s_ref)

    log_g = jnp.log1p(-jnp.exp2(-5.0 - jnp.full((1, 1), hd, jnp.int32).astype(F32)))
    row = lax.broadcasted_iota(jnp.int32, (chunk, 1), 0).astype(F32)
    q_decay = jnp.exp(log_g * (row + 1.0))
    k_decay = jnp.exp(log_g * (chunk - 1.0 - row))
    chunk_decay = jnp.exp(log_g * chunk)
    diff = (lax.broadcasted_iota(jnp.int32, (chunk, chunk), 0)
            - lax.broadcasted_iota(jnp.int32, (chunk, chunk), 1)).astype(F32)
    intra = jnp.where(diff >= 0, jnp.exp(log_g * jnp.maximum(diff, 0.0)), 0.0)

    q = q_ref[...].astype(BF16)
    k = k_ref[...]
    v = v_ref[...]
    state = s_ref[...]
    scores = lax.dot_general(q, k.astype(BF16), (((1,), (1,)), ((), ())), preferred_element_type=F32)
    inner = _bdot((scores * intra).astype(BF16), v)
    cross = _bdot(q, state.astype(BF16)) * q_decay
    update = lax.dot_general((k * k_decay).astype(BF16), v, (((0,), (0,)), ((), ())),
                             preferred_element_type=F32)
    s_ref[...] = state * chunk_decay + update

    o = inner + cross
    mu = jnp.mean(o, axis=-1, keepdims=True)
    d = o - mu
    var = jnp.mean(d * d, axis=-1, keepdims=True)
    o_ref[...] = (g_ref[...] * (d * lax.rsqrt(var + LN_EPS))).astype(BF16)


def _ret_call(qk, v, gate, bsz, seq, chunk):
    qk4 = qk.reshape(2, bsz, seq, RET_HEADS * RET_QK_DIM)
    v3 = v.reshape(bsz, seq, RET_V_WIDTH)
    g3 = gate.reshape(bsz, seq, RET_V_WIDTH)
    out = pl.pallas_call(
        functools.partial(_ret_kernel, chunk=chunk),
        grid=(bsz, RET_HEADS, seq // chunk),
        in_specs=[
            pl.BlockSpec((None, None, chunk, RET_QK_DIM), lambda b, h, n: (0, b, n, h)),
            pl.BlockSpec((None, None, chunk, RET_QK_DIM), lambda b, h, n: (1, b, n, h)),
            pl.BlockSpec((None, chunk, RET_V_DIM), lambda b, h, n: (b, n, h)),
            pl.BlockSpec((None, chunk, RET_V_DIM), lambda b, h, n: (b, n, h)),
        ],
        out_specs=pl.BlockSpec((None, chunk, RET_V_DIM), lambda b, h, n: (b, n, h)),
        out_shape=jax.ShapeDtypeStruct((bsz, seq, RET_V_WIDTH), BF16),
        scratch_shapes=[pltpu.VMEM((RET_QK_DIM, RET_V_DIM), F32)],
        compiler_params=_params("parallel", "parallel", "arbitrary"),
        name="retention",
    )(qk4, qk4, v3, g3)
    return out.reshape(bsz * seq, RET_V_WIDTH)


def _top2_combine(logits):
    lane = lax.broadcasted_iota(jnp.int32, logits.shape, 1)
    neg = jnp.float32(-jnp.inf)
    lg = jnp.where(lane < N_EXPERTS, logits, neg)
    m1 = jnp.max(lg, axis=-1, keepdims=True)
    i1 = jnp.min(jnp.where(lg == m1, lane, LANES), axis=-1, keepdims=True)
    lg2 = jnp.where(lane == i1, neg, lg)
    m2 = jnp.max(lg2, axis=-1, keepdims=True)
    i2 = jnp.min(jnp.where(lg2 == m2, lane, LANES), axis=-1, keepdims=True)
    e = jnp.exp(m2 - m1)
    w1 = 1.0 / (1.0 + e)
    w2 = e / (1.0 + e)
    return jnp.where(lane == i1, w1, 0.0) + jnp.where(lane == i2, w2, 0.0)


def _mix_kernel(*refs, router):
    if router:
        (uc_ref, og_ref, gt_ref, x_ref, ada_ref, wc_ref, wr_ref, wo_ref, g_ref, b_ref, wrt_ref,
         xo_ref, ho_ref, cmb_ref) = refs
    else:
        (uc_ref, og_ref, gt_ref, x_ref, ada_ref, wc_ref, wr_ref, wo_ref, g_ref, b_ref,
         xo_ref, ho_ref) = refs
    d = x_ref.shape[-1]
    y_conv = _bdot(uc_ref[...], wc_ref[...])
    y_ret = _bdot(og_ref[...], wr_ref[...])
    merged = gt_ref[:, 0:d] * y_conv + gt_ref[:, d:2 * d] * y_ret
    mix = _bdot(merged.astype(BF16), wo_ref[...])
    gate1 = ada_ref[0, 2:3, :]
    shift2 = ada_ref[0, 3:4, :]
    scale2 = ada_ref[0, 4:5, :]
    xn = _layer_norm_rows(DEEPNORM_ALPHA * x_ref[...] + gate1 * mix, g_ref[...], b_ref[...])
    xo_ref[...] = xn
    h2 = xn * (1.0 + scale2) + shift2
    ho_ref[...] = h2.astype(BF16)
    if router:
        cmb_ref[...] = _top2_combine(_dot3(h2, wrt_ref[...]))


def _mix_call(uc, og, gates, xf, ada_l, wc, wr, wo, ln_g, ln_b, w_router, seq, tm):
    n, d = xf.shape
    per = seq // tm
    router = w_router is not None
    row = lambda i: (i, 0)
    const = lambda i: (0, 0)
    in_specs = [
        pl.BlockSpec((tm, d), row),
        pl.BlockSpec((tm, RET_V_WIDTH), row),
        pl.BlockSpec((tm, 2 * d), row),
        pl.BlockSpec((tm, d), row),
        pl.BlockSpec((1, N_ADA, d), lambda i: (i // per, 0, 0)),
        pl.BlockSpec((d, d), const),
        pl.BlockSpec((RET_V_WIDTH, d), const),
        pl.BlockSpec((d, d), const),
        pl.BlockSpec((1, d), const),
        pl.BlockSpec((1, d), const),
    ]
    args = [uc, og, gates, xf, ada_l, wc, wr, wo, ln_g.reshape(1, d), ln_b.reshape(1, d)]
    out_specs = [pl.BlockSpec((tm, d), row), pl.BlockSpec((tm, d), row)]
    out_shape = [jax.ShapeDtypeStruct((n, d), F32), jax.ShapeDtypeStruct((n, d), BF16)]
    if router:
        wrt = jnp.zeros((d, LANES), F32).at[:, :N_EXPERTS].set(w_router)
        in_specs.append(pl.BlockSpec((d, LANES), const))
        args.append(wrt)
        out_specs.append(pl.BlockSpec((tm, LANES), row))
        out_shape.append(jax.ShapeDtypeStruct((n, LANES), F32))
    return pl.pallas_call(
        functools.partial(_mix_kernel, router=router),
        grid=(n // tm,),
        in_specs=in_specs,
        out_specs=out_specs,
        out_shape=out_shape,
        compiler_params=_params("parallel"),
        name="mix_out_router" if router else "mix_out",
    )(*args)


def _ffn_kernel(*refs, moe, emit_h):
    refs = list(refs)
    h_ref, wg_ref, wu_ref, wd_ref = refs[:4]
    rest = refs[4:]
    cmb_ref = rest.pop(0) if moe else None
    x_ref, ada_ref = rest[:2]
    rest = rest[2:]
    adan_ref = rest.pop(0) if emit_h else None
    g_ref, b_ref, xo_ref = rest[:3]
    rest = rest[3:]
    ho_ref = rest.pop(0) if emit_h else None
    acc_ref = rest.pop(0)

    e = pl.program_id(1)
    c = pl.program_id(2)
    first = jnp.logical_and(e == 0, c == 0)
    last = jnp.logical_and(e == pl.num_programs(1) - 1, c == pl.num_programs(2) - 1)

    @pl.when(first)
    def _():
        acc_ref[...] = jnp.zeros_like(acc_ref)

    h = h_ref[...]
    gate = _bdot(h, wg_ref[0])
    up = _bdot(h, wu_ref[0])
    part = _bdot((gate * _sigmoid(gate) * up).astype(BF16), wd_ref[0])
    if moe:
        lane = lax.broadcasted_iota(jnp.int32, cmb_ref.shape, 1)
        weight = jnp.sum(jnp.where(lane == e, cmb_ref[...], 0.0), axis=-1, keepdims=True)
        part = weight * part
    acc_ref[...] += part

    @pl.when(last)
    def _():
        gate2 = ada_ref[0, 5:6, :]
        xn = _layer_norm_rows(DEEPNORM_ALPHA * x_ref[...] + gate2 * acc_ref[...], g_ref[...], b_ref[...])
        xo_ref[...] = xn
        if emit_h:
            shift1 = adan_ref[0, 0:1, :]
            scale1 = adan_ref[0, 1:2, :]
            ho_ref[...] = (xn * (1.0 + scale1) + shift1).astype(BF16)


def _ffn_call(h, wg, wu, wd, combine, xf, ada_l, ada_next, ln_g, ln_b, seq, tm, fc):
    n, d = xf.shape
    n_exp, _, dff = wg.shape
    per = seq // tm
    moe = combine is not None
    emit_h = ada_next is not None
    row = lambda i, e, c: (i, 0)
    const = lambda i, e, c: (0, 0)
    ada_map = lambda i, e, c: (i // per, 0, 0)
    in_specs = [
        pl.BlockSpec((tm, d), row),
        pl.BlockSpec((1, d, fc), lambda i, e, c: (e, 0, c)),
        pl.BlockSpec((1, d, fc), lambda i, e, c: (e, 0, c)),
        pl.BlockSpec((1, fc, d), lambda i, e, c: (e, c, 0)),
    ]
    args = [h, wg, wu, wd]
    if moe:
        in_specs.append(pl.BlockSpec((tm, LANES), row))
        args.append(combine)
    in_specs += [pl.BlockSpec((tm, d), row), pl.BlockSpec((1, N_ADA, d), ada_map)]
    args += [xf, ada_l]
    if emit_h:
        in_specs.append(pl.BlockSpec((1, N_ADA, d), ada_map))
        args.append(ada_next)
    in_specs += [pl.BlockSpec((1, d), const), pl.BlockSpec((1, d), const)]
    args += [ln_g.reshape(1, d), ln_b.reshape(1, d)]
    out_specs = [pl.BlockSpec((tm, d), row)]
    out_shape = [jax.ShapeDtypeStruct((n, d), F32)]
    if emit_h:
        out_specs.append(pl.BlockSpec((tm, d), row))
        out_shape.append(jax.ShapeDtypeStruct((n, d), BF16))
    outs = pl.pallas_call(
        functools.partial(_ffn_kernel, moe=moe, emit_h=emit_h),
        grid=(n // tm, n_exp, dff // fc),
        in_specs=in_specs,
        out_specs=out_specs,
        out_shape=out_shape,
        scratch_shapes=[pltpu.VMEM((tm, d), F32)],
        compiler_params=_params("parallel", "arbitrary", "arbitrary"),
        name="moe_ffn" if moe else "dense_ffn",
    )(*args)
    return outs if emit_h else (outs[0], None)


def kernel(x, c, positions, w_ada, b_ada, w_in, w_dw, b_dw, ln_conv_g, ln_conv_b, w_conv_o, w_ret_o, w_out, ln1_g, ln1_b, ffn_w_gate, ffn_w_up, ffn_w_down, moe_w_router, moe_w_gate, moe_w_up, moe_w_down, ln2_g, ln2_b):
    bsz, seq, d = x.shape
    n = bsz * seq
    tm = 512
    tm_mix = 256
    fc = D_FF // 2
    tt = 128

    xf = x.reshape(n, d)
    ada = _ada_call(c, w_ada, b_ada)
    cos, sin = _rope_call(positions, tm)
    h = _modulate_call(xf, ada[0], seq, tm)

    for l in range(DEPTH):
        w_in_b = w_in[l].astype(BF16)
        u = _glu_call(h, w_in_b, tm)
        qk = _qk_call(h, w_in_b, cos, sin, tm)
        v = _act_call(h, w_in_b, 4, 2, None, BF16, tm, "proj_v")
        ret_gate = _act_call(h, w_in_b, 6, 2, "silu", F32, tm, "proj_ret_gate")
        gates = _act_call(h, w_in_b, 8, 2, "sigmoid", F32, tm, "proj_gates")

        uc = _conv_call(u.reshape(bsz, seq, d), w_dw[l], b_dw[l], ln_conv_g[l], ln_conv_b[l], tt)
        og = _ret_call(qk, v, ret_gate, bsz, seq, RET_CHUNK)

        i = l // 2
        is_moe = l % 2 == 1
        mixed = _mix_call(
            uc.reshape(n, d), og, gates, xf, ada[l],
            w_conv_o[l].astype(BF16), w_ret_o[l].astype(BF16), w_out[l].astype(BF16),
            ln1_g[l], ln1_b[l], moe_w_router[i] if is_moe else None, seq, tm_mix)
        xf, h2 = mixed[0], mixed[1]
        ada_next = ada[l + 1] if l + 1 < DEPTH else None
        if is_moe:
            xf, h = _ffn_call(h2, moe_w_gate[i].astype(BF16), moe_w_up[i].astype(BF16),
                              moe_w_down[i].astype(BF16), mixed[2], xf, ada[l], ada_next,
                              ln2_g[l], ln2_b[l], seq, tm, fc)
        else:
            xf, h = _ffn_call(h2, ffn_w_gate[i][None].astype(BF16), ffn_w_up[i][None].astype(BF16),
                              ffn_w_down[i][None].astype(BF16), None, xf, ada[l], ada_next,
                              ln2_g[l], ln2_b[l], seq, tm, fc)
    return xf.reshape(bsz, seq, d)
```

```python
import functools
import math

import jax
import jax.numpy as jnp
from jax import lax
from jax.experimental import pallas as pl
from jax.experimental.pallas import tpu as pltpu

D_MODEL = 1024
DEPTH = 4
CONV_KERNEL = 31
RET_QK_DIM = 256
RET_V_DIM = 512
RET_HEADS = 4
RET_V_WIDTH = RET_HEADS * RET_V_DIM
RET_CHUNK = 128
ROPE_BASE = 10000.0
D_FF = 2816
N_EXPERTS = 8
N_ADA = 6
DEEPNORM_ALPHA = (2 * DEPTH) ** 0.25
LN_EPS = 1e-5

LANES = 128
SUBLANES = 8
CONV_HALO = 32
CONV_ROWS = 32
VMEM_LIMIT = 56 * 1024 * 1024

R_E1, R_E2, R_RANK1, R_RANK2, R_W1, R_W2 = range(6)

F32 = jnp.float32
BF16 = jnp.bfloat16


def _params(*sem):
    return pltpu.CompilerParams(dimension_semantics=sem, vmem_limit_bytes=VMEM_LIMIT)


def _bdot(a, b):
    return jnp.dot(a, b, preferred_element_type=F32)


def _split_bf16(a):
    hi = a.astype(BF16)
    lo = (a - hi.astype(F32)).astype(BF16)
    return hi, lo


def _dot3(a, b):
    a_hi, a_lo = _split_bf16(a)
    b_hi, b_lo = _split_bf16(b)
    return _bdot(a_hi, b_hi) + (_bdot(a_hi, b_lo) + _bdot(a_lo, b_hi))


def _sigmoid(x):
    return jax.nn.sigmoid(x)


def _layer_norm_rows(y, g, b):
    mu = jnp.mean(y, axis=-1, keepdims=True)
    d = y - mu
    var = jnp.mean(d * d, axis=-1, keepdims=True)
    return d * lax.rsqrt(var + LN_EPS) * g + b


def _ada_kernel(c_ref, w_ref, b_ref, o_ref):
    c = c_ref[...]
    o_ref[0] = _dot3(c * _sigmoid(c), w_ref[0]) + b_ref[0]


def _ada_call(c, w_ada, b_ada):
    depth, d, width = w_ada.shape
    bsz = c.shape[0]
    ncol = width // d
    out = pl.pallas_call(
        _ada_kernel,
        grid=(depth, ncol),
        in_specs=[
            pl.BlockSpec((bsz, d), lambda l, j: (0, 0)),
            pl.BlockSpec((1, d, d), lambda l, j: (l, 0, j)),
            pl.BlockSpec((1, 1, d), lambda l, j: (l, 0, j)),
        ],
        out_specs=pl.BlockSpec((1, bsz, d), lambda l, j: (l, 0, j)),
        out_shape=jax.ShapeDtypeStruct((depth, bsz, width), F32),
        compiler_params=_params("parallel", "parallel"),
        name="ada",
    )(c, w_ada, b_ada.reshape(depth, 1, width))
    return out.reshape(depth, bsz, N_ADA, d)


def _rope_kernel(pos_ref, inv_ref, cos_ref, sin_ref):
    ang = pos_ref[...].astype(F32) * inv_ref[...]
    cos_ref[...] = jnp.cos(ang)
    sin_ref[...] = jnp.sin(ang)


def _rope_call(positions, tm):
    n = positions.size
    half = RET_QK_DIM // 2
    inv_freq = ROPE_BASE ** (-jnp.arange(half, dtype=F32) / half)
    return pl.pallas_call(
        _rope_kernel,
        grid=(n // tm,),
        in_specs=[
            pl.BlockSpec((tm, 1), lambda i: (i, 0)),
            pl.BlockSpec((1, half), lambda i: (0, 0)),
        ],
        out_specs=[pl.BlockSpec((tm, half), lambda i: (i, 0))] * 2,
        out_shape=[jax.ShapeDtypeStruct((n, half), F32)] * 2,
        compiler_params=_params("parallel"),
        name="rope",
    )(positions.reshape(n, 1), inv_freq.reshape(1, half))


def _modulate_kernel(x_ref, ada_ref, h_ref):
    shift = ada_ref[0, 0:1, :]
    scale = ada_ref[0, 1:2, :]
    h_ref[...] = (x_ref[...] * (1.0 + scale) + shift).astype(BF16)


def _modulate_call(xf, ada_l, seq, tm):
    n, d = xf.shape
    per = seq // tm
    return pl.pallas_call(
        _modulate_kernel,
        grid=(n // tm,),
        in_specs=[
            pl.BlockSpec((tm, d), lambda i: (i, 0)),
            pl.BlockSpec((1, N_ADA, d), lambda i: (i // per, 0, 0)),
        ],
        out_specs=pl.BlockSpec((tm, d), lambda i: (i, 0)),
        out_shape=jax.ShapeDtypeStruct((n, d), BF16),
        compiler_params=_params("parallel"),
        name="modulate",
    )(xf, ada_l)


def _glu_kernel(h_ref, wa_ref, wb_ref, u_ref):
    h = h_ref[...]
    a = _bdot(h, wa_ref[...])
    b = _bdot(h, wb_ref[...])
    u_ref[...] = a * _sigmoid(b)


def _glu_call(h, w_in_b, tm):
    n, d = h.shape
    return pl.pallas_call(
        _glu_kernel,
        grid=(n // tm,),
        in_specs=[
            pl.BlockSpec((tm, d), lambda i: (i, 0)),
            pl.BlockSpec((d, d), lambda i: (0, 0)),
            pl.BlockSpec((d, d), lambda i: (0, 1)),
        ],
        out_specs=pl.BlockSpec((tm, d), lambda i: (i, 0)),
        out_shape=jax.ShapeDtypeStruct((n, d), F32),
        compiler_params=_params("parallel"),
        name="proj_glu",
    )(h, w_in_b, w_in_b)


def _qk_kernel(h_ref, w_ref, cos_ref, sin_ref, o_ref):
    acc = _bdot(h_ref[...], w_ref[...])
    scale = jnp.where(pl.program_id(0) == 0, 1.0, RET_QK_DIM ** -0.5).astype(F32)
    cos = cos_ref[...] * scale
    sin = sin_ref[...] * scale
    half = RET_QK_DIM // 2
    for hd in range(RET_HEADS):
        lo = hd * RET_QK_DIM
        x1 = acc[:, lo:lo + half]
        x2 = acc[:, lo + half:lo + RET_QK_DIM]
        o_ref[0, :, lo:lo + half] = x1 * cos - x2 * sin
        o_ref[0, :, lo + half:lo + RET_QK_DIM] = x2 * cos + x1 * sin


def _qk_call(h, w_in_b, cos, sin, tm):
    n, d = h.shape
    half = RET_QK_DIM // 2
    return pl.pallas_call(
        _qk_kernel,
        grid=(2, n // tm),
        in_specs=[
            pl.BlockSpec((tm, d), lambda j, i: (i, 0)),
            pl.BlockSpec((d, d), lambda j, i: (0, 2 + j)),
            pl.BlockSpec((tm, half), lambda j, i: (i, 0)),
            pl.BlockSpec((tm, half), lambda j, i: (i, 0)),
        ],
        out_specs=pl.BlockSpec((1, tm, d), lambda j, i: (j, i, 0)),
        out_shape=jax.ShapeDtypeStruct((2, n, d), F32),
        compiler_params=_params("parallel", "parallel"),
        name="proj_qk",
    )(h, w_in_b, cos, sin)


def _act_kernel(h_ref, w_ref, o_ref, *, act):
    acc = _bdot(h_ref[...], w_ref[...])
    if act == "silu":
        acc = acc * _sigmoid(acc)
    elif act == "sigmoid":
        acc = _sigmoid(acc)
    o_ref[...] = acc.astype(o_ref.dtype)


def _act_call(h, w_in_b, col0, ncols, act, dtype, tm, name):
    n, d = h.shape
    return pl.pallas_call(
        functools.partial(_act_kernel, act=act),
        grid=(ncols, n // tm),
        in_specs=[
            pl.BlockSpec((tm, d), lambda j, i: (i, 0)),
            pl.BlockSpec((d, d), lambda j, i: (0, col0 + j)),
        ],
        out_specs=pl.BlockSpec((tm, d), lambda j, i: (i, j)),
        out_shape=jax.ShapeDtypeStruct((n, ncols * d), dtype),
        compiler_params=_params("parallel", "parallel"),
        name=name,
    )(h, w_in_b)


def _conv_kernel(cur_ref, prev_ref, w_ref, bias_ref, g_ref, b_ref, o_ref, buf_ref, *, tt):
    ch = cur_ref.shape[-1]
    first = pl.program_id(1) == 0
    buf_ref[0, 0:CONV_HALO, :] = jnp.where(first, 0.0, prev_ref[0])
    buf_ref[0, CONV_HALO:, :] = cur_ref[0]

    sub = lax.broadcasted_iota(jnp.int32, (SUBLANES, ch), 0)
    n_shift_tiles = (tt + CONV_HALO) // SUBLANES - 1
    for s in range(1, SUBLANES):
        own = sub < SUBLANES - s
        cur = pltpu.roll(buf_ref[0, 0:SUBLANES, :], SUBLANES - s, 0)
        for j in range(n_shift_tiles):
            nxt = pltpu.roll(buf_ref[0, (j + 1) * SUBLANES:(j + 2) * SUBLANES, :], SUBLANES - s, 0)
            buf_ref[s, j * SUBLANES:(j + 1) * SUBLANES, :] = jnp.where(own, cur, nxt)
            cur = nxt

    bias = bias_ref[...]
    g = g_ref[...]
    b = b_ref[...]
    base = CONV_HALO - (CONV_KERNEL - 1)

    def block(r, carry):
        r0 = pl.multiple_of(r * CONV_ROWS, CONV_ROWS)
        acc = jnp.zeros((CONV_ROWS, ch), F32)
        for k in range(CONV_KERNEL):
            a, s = divmod(base + k, SUBLANES)
            acc = acc + buf_ref[s, pl.ds(r0 + a * SUBLANES, CONV_ROWS), :] * w_ref[k:k + 1, :]
        y = _layer_norm_rows(acc + bias, g, b)
        o_ref[0, pl.ds(r0, CONV_ROWS), :] = (y * _sigmoid(y)).astype(BF16)
        return carry

    lax.fori_loop(0, tt // CONV_ROWS, block, 0)


def _conv_call(u, w_dw, b_dw, ln_g, ln_b, tt):
    bsz, seq, ch = u.shape
    per = tt // CONV_HALO
    return pl.pallas_call(
        functools.partial(_conv_kernel, tt=tt),
        grid=(bsz, seq // tt),
        in_specs=[
            pl.BlockSpec((1, tt, ch), lambda b, t: (b, t, 0)),
            pl.BlockSpec((1, CONV_HALO, ch), lambda b, t: (b, jnp.maximum(t * per - 1, 0), 0)),
            pl.BlockSpec((CONV_KERNEL, ch), lambda b, t: (0, 0)),
            pl.BlockSpec((1, ch), lambda b, t: (0, 0)),
            pl.BlockSpec((1, ch), lambda b, t: (0, 0)),
            pl.BlockSpec((1, ch), lambda b, t: (0, 0)),
        ],
        out_specs=pl.BlockSpec((1, tt, ch), lambda b, t: (b, t, 0)),
        out_shape=jax.ShapeDtypeStruct((bsz, seq, ch), BF16),
        scratch_shapes=[pltpu.VMEM((SUBLANES, CONV_HALO + tt, ch), F32)],
        compiler_params=_params("parallel", "parallel"),
        name="conv_ln",
    )(u, u, w_dw, b_dw.reshape(1, ch), ln_g.reshape(1, ch), ln_b.reshape(1, ch))


def _ret_kernel(q_ref, k_ref, v_ref, g_ref, o_ref, s_ref, *, chunk):
    @pl.when(pl.program_id(1) == 0)
    def _():
        s_ref[...] = jnp.zeros_like(s_ref)

    row = lax.broadcasted_iota(jnp.int32, (chunk, 1), 0).astype(F32)
    diff = (lax.broadcasted_iota(jnp.int32, (chunk, chunk), 0)
            - lax.broadcasted_iota(jnp.int32, (chunk, chunk), 1)).astype(F32)
    causal = diff >= 0
    lag = jnp.maximum(diff, 0.0)
    for hd in range(RET_HEADS):
        log_g = math.log1p(-(2.0 ** (-5 - hd)))
        q_decay = jnp.exp(log_g * (row + 1.0))
        k_decay = jnp.exp(log_g * (chunk - 1.0 - row))
        chunk_decay = math.exp(log_g * chunk)
        intra = jnp.where(causal, jnp.exp(log_g * lag), 0.0)

        qs = slice(hd * RET_QK_DIM, (hd + 1) * RET_QK_DIM)
        vs = slice(hd * RET_V_DIM, (hd + 1) * RET_V_DIM)
        q = q_ref[:, qs].astype(BF16)
        k = k_ref[:, qs]
        v = v_ref[:, vs]
        state = s_ref[hd]
        scores = lax.dot_general(q, k.astype(BF16), (((1,), (1,)), ((), ())), preferred_element_type=F32)
        inner = _bdot((scores * intra).astype(BF16), v)
        cross = _bdot(q, state.astype(BF16)) * q_decay
        update = lax.dot_general((k * k_decay).astype(BF16), v, (((0,), (0,)), ((), ())),
                                 preferred_element_type=F32)
        s_ref[hd] = state * chunk_decay + update

        o = inner + cross
        mu = jnp.mean(o, axis=-1, keepdims=True)
        d = o - mu
        var = jnp.mean(d * d, axis=-1, keepdims=True)
        o_ref[:, vs] = (g_ref[:, vs] * (d * lax.rsqrt(var + LN_EPS))).astype(BF16)


def _ret_call(qk, v, gate, bsz, seq, chunk):
    width = RET_HEADS * RET_QK_DIM
    qk4 = qk.reshape(2, bsz, seq, width)
    v3 = v.reshape(bsz, seq, RET_V_WIDTH)
    g3 = gate.reshape(bsz, seq, RET_V_WIDTH)
    out = pl.pallas_call(
        functools.partial(_ret_kernel, chunk=chunk),
        grid=(bsz, seq // chunk),
        in_specs=[
            pl.BlockSpec((None, None, chunk, width), lambda b, n: (0, b, n, 0)),
            pl.BlockSpec((None, None, chunk, width), lambda b, n: (1, b, n, 0)),
            pl.BlockSpec((None, chunk, RET_V_WIDTH), lambda b, n: (b, n, 0)),
            pl.BlockSpec((None, chunk, RET_V_WIDTH), lambda b, n: (b, n, 0)),
        ],
        out_specs=pl.BlockSpec((None, chunk, RET_V_WIDTH), lambda b, n: (b, n, 0)),
        out_shape=jax.ShapeDtypeStruct((bsz, seq, RET_V_WIDTH), BF16),
        scratch_shapes=[pltpu.VMEM((RET_HEADS, RET_QK_DIM, RET_V_DIM), F32)],
        compiler_params=_params("parallel", "arbitrary"),
        name="retention",
    )(qk4, qk4, v3, g3)
    return out.reshape(bsz * seq, RET_V_WIDTH)


def _route(logits, carry_ref):
    rows = logits.shape[0]
    lane = lax.broadcasted_iota(jnp.int32, logits.shape, 1)
    neg = jnp.float32(-jnp.inf)
    lg = jnp.where(lane < N_EXPERTS, logits, neg)
    m1 = jnp.max(lg, axis=-1, keepdims=True)
    i1 = jnp.min(jnp.where(lg == m1, lane, LANES), axis=-1, keepdims=True)
    lg2 = jnp.where(lane == i1, neg, lg)
    m2 = jnp.max(lg2, axis=-1, keepdims=True)
    i2 = jnp.min(jnp.where(lg2 == m2, lane, LANES), axis=-1, keepdims=True)
    e = jnp.exp(m2 - m1)
    w1 = 1.0 / (1.0 + e)
    w2 = e / (1.0 + e)

    sel = jnp.where(jnp.logical_or(lane == i1, lane == i2), 1.0, 0.0)
    tri = jnp.where(lax.broadcasted_iota(jnp.int32, (rows, rows), 0)
                    >= lax.broadcasted_iota(jnp.int32, (rows, rows), 1), 1.0, 0.0).astype(BF16)
    incl = _bdot(tri, sel.astype(BF16))
    rank = carry_ref[...] + incl - sel
    carry_ref[...] = carry_ref[...] + incl[rows - 1:rows, :]
    r1 = jnp.sum(jnp.where(lane == i1, rank, 0.0), axis=-1, keepdims=True)
    r2 = jnp.sum(jnp.where(lane == i2, rank, 0.0), axis=-1, keepdims=True)

    rec = jnp.zeros(logits.shape, F32)
    for idx, val in ((R_E1, i1.astype(F32)), (R_E2, i2.astype(F32)), (R_RANK1, r1), (R_RANK2, r2),
                     (R_W1, w1), (R_W2, w2)):
        rec = jnp.where(lane == idx, val, rec)
    return rec


def _mix_kernel(*refs, router):
    if router:
        (uc_ref, og_ref, gt_ref, x_ref, ada_ref, wc_ref, wr_ref, wo_ref, g_ref, b_ref, wrt_ref,
         xo_ref, ho_ref, rec_ref, cnt_ref, carry_ref) = refs
    else:
        (uc_ref, og_ref, gt_ref, x_ref, ada_ref, wc_ref, wr_ref, wo_ref, g_ref, b_ref,
         xo_ref, ho_ref) = refs
    d = x_ref.shape[-1]
    y_conv = _bdot(uc_ref[...], wc_ref[...])
    y_ret = _bdot(og_ref[...], wr_ref[...])
    merged = gt_ref[:, 0:d] * y_conv + gt_ref[:, d:2 * d] * y_ret
    mix = _bdot(merged.astype(BF16), wo_ref[...])
    gate1 = ada_ref[0, 2:3, :]
    shift2 = ada_ref[0, 3:4, :]
    scale2 = ada_ref[0, 4:5, :]
    xn = _layer_norm_rows(DEEPNORM_ALPHA * x_ref[...] + gate1 * mix, g_ref[...], b_ref[...])
    xo_ref[...] = xn
    h2 = xn * (1.0 + scale2) + shift2
    ho_ref[...] = h2.astype(ho_ref.dtype)
    if router:
        @pl.when(pl.program_id(0) == 0)
        def _():
            carry_ref[...] = jnp.zeros_like(carry_ref)

        rec_ref[...] = _route(_dot3(h2, wrt_ref[...]), carry_ref)
        cnt_ref[...] = carry_ref[...]


def _mix_call(uc, og, gates, xf, ada_l, wc, wr, wo, ln_g, ln_b, w_router, seq, tm):
    n, d = xf.shape
    per = seq // tm
    router = w_router is not None
    row = lambda i: (i, 0)
    const = lambda i: (0, 0)
    in_specs = [
        pl.BlockSpec((tm, d), row),
        pl.BlockSpec((tm, RET_V_WIDTH), row),
        pl.BlockSpec((tm, 2 * d), row),
        pl.BlockSpec((tm, d), row),
        pl.BlockSpec((1, N_ADA, d), lambda i: (i // per, 0, 0)),
        pl.BlockSpec((d, d), const),
        pl.BlockSpec((RET_V_WIDTH, d), const),
        pl.BlockSpec((d, d), const),
        pl.BlockSpec((1, d), const),
        pl.BlockSpec((1, d), const),
    ]
    args = [uc, og, gates, xf, ada_l, wc, wr, wo, ln_g.reshape(1, d), ln_b.reshape(1, d)]
    out_specs = [pl.BlockSpec((tm, d), row), pl.BlockSpec((tm, d), row)]
    out_shape = [jax.ShapeDtypeStruct((n, d), F32), jax.ShapeDtypeStruct((n, d), F32 if router else BF16)]
    scratch = []
    if router:
        wrt = jnp.zeros((d, LANES), F32).at[:, :N_EXPERTS].set(w_router)
        in_specs.append(pl.BlockSpec((d, LANES), const))
        args.append(wrt)
        out_specs += [pl.BlockSpec((tm, LANES), row), pl.BlockSpec((1, LANES), const)]
        out_shape += [jax.ShapeDtypeStruct((n, LANES), F32), jax.ShapeDtypeStruct((1, LANES), F32)]
        scratch = [pltpu.VMEM((1, LANES), F32)]
    return pl.pallas_call(
        functools.partial(_mix_kernel, router=router),
        grid=(n // tm,),
        in_specs=in_specs,
        out_specs=out_specs,
        out_shape=out_shape,
        scratch_shapes=scratch,
        compiler_params=_params("arbitrary" if router else "parallel"),
        name="mix_out_router" if router else "mix_out",
    )(*args)


def _residual_epilogue(ff, x_ref, ada_ref, adan_ref, g_ref, b_ref, xo_ref, ho_ref):
    gate2 = ada_ref[0, 5:6, :]
    xn = _layer_norm_rows(DEEPNORM_ALPHA * x_ref[...] + gate2 * ff, g_ref[...], b_ref[...])
    xo_ref[...] = xn
    if ho_ref is not None:
        shift1 = adan_ref[0, 0:1, :]
        scale1 = adan_ref[0, 1:2, :]
        ho_ref[...] = (xn * (1.0 + scale1) + shift1).astype(BF16)


def _swiglu_cols(h, wg_ref, wu_ref, wd_ref, lo, hi):
    gate = _bdot(h, wg_ref[0, :, lo:hi])
    up = _bdot(h, wu_ref[0, :, lo:hi])
    return _bdot((gate * _sigmoid(gate) * up).astype(BF16), wd_ref[0, lo:hi, :])


def _ffn_kernel(*refs, emit_h):
    if emit_h:
        h_ref, wg_ref, wu_ref, wd_ref, x_ref, ada_ref, adan_ref, g_ref, b_ref, xo_ref, ho_ref, acc_ref = refs
    else:
        h_ref, wg_ref, wu_ref, wd_ref, x_ref, ada_ref, g_ref, b_ref, xo_ref, acc_ref = refs
        adan_ref = ho_ref = None
    c = pl.program_id(1)

    @pl.when(c == 0)
    def _():
        acc_ref[...] = jnp.zeros_like(acc_ref)

    acc_ref[...] += _swiglu_cols(h_ref[...], wg_ref, wu_ref, wd_ref, 0, wg_ref.shape[-1])

    @pl.when(c == pl.num_programs(1) - 1)
    def _():
        _residual_epilogue(acc_ref[...], x_ref, ada_ref, adan_ref, g_ref, b_ref, xo_ref, ho_ref)


def _ffn_call(h, wg, wu, wd, xf, ada_l, ada_next, ln_g, ln_b, seq, tm, fc):
    n, d = xf.shape
    dff = wg.shape[-1]
    per = seq // tm
    emit_h = ada_next is not None
    row = lambda i, c: (i, 0)
    const = lambda i, c: (0, 0)
    ada_map = lambda i, c: (i // per, 0, 0)
    in_specs = [
        pl.BlockSpec((tm, d), row),
        pl.BlockSpec((1, d, fc), lambda i, c: (0, 0, c)),
        pl.BlockSpec((1, d, fc), lambda i, c: (0, 0, c)),
        pl.BlockSpec((1, fc, d), lambda i, c: (0, c, 0)),
        pl.BlockSpec((tm, d), row),
        pl.BlockSpec((1, N_ADA, d), ada_map),
    ]
    args = [h, wg, wu, wd, xf, ada_l]
    if emit_h:
        in_specs.append(pl.BlockSpec((1, N_ADA, d), ada_map))
        args.append(ada_next)
    in_specs += [pl.BlockSpec((1, d), const), pl.BlockSpec((1, d), const)]
    args += [ln_g.reshape(1, d), ln_b.reshape(1, d)]
    out_specs = [pl.BlockSpec((tm, d), row)]
    out_shape = [jax.ShapeDtypeStruct((n, d), F32)]
    if emit_h:
        out_specs.append(pl.BlockSpec((tm, d), row))
        out_shape.append(jax.ShapeDtypeStruct((n, d), BF16))
    outs = pl.pallas_call(
        functools.partial(_ffn_kernel, emit_h=emit_h),
        grid=(n // tm, dff // fc),
        in_specs=in_specs,
        out_specs=out_specs,
        out_shape=out_shape,
        scratch_shapes=[pltpu.VMEM((tm, d), F32)],
        compiler_params=_params("parallel", "arbitrary"),
        name="dense_ffn",
    )(*args)
    return outs if emit_h else (outs[0], None)


def _row_copy(src, src_row, dst, dst_row, sem):
    return pltpu.make_async_copy(src.at[pl.ds(src_row, 1)], dst.at[pl.ds(dst_row, 1)], sem)


def _scatter_kernel(zt_ref, p1_ref, p2_ref, hf_hbm, xs_hbm, zero_ref, sem):
    i = pl.program_id(0)
    tb = p1_ref.shape[0]
    tile = zero_ref.shape[0]

    def zero_copy(j):
        return pltpu.make_async_copy(zero_ref, xs_hbm.at[pl.ds(zt_ref[j] * tile, tile)], sem)

    @pl.when(i == 0)
    def _():
        zero_ref[...] = jnp.zeros_like(zero_ref)
        for j in range(zt_ref.shape[0]):
            pl.when(zt_ref[j] >= 0)(lambda j=j: zero_copy(j).start())
        for j in range(zt_ref.shape[0]):
            pl.when(zt_ref[j] >= 0)(lambda j=j: zero_copy(j).wait())

    def start(k, carry):
        t = i * tb + k
        _row_copy(hf_hbm, t, xs_hbm, p1_ref[k], sem).start()
        _row_copy(hf_hbm, t, xs_hbm, p2_ref[k], sem).start()
        return carry

    def wait(k, carry):
        t = i * tb + k
        _row_copy(hf_hbm, t, xs_hbm, p1_ref[k], sem).wait()
        _row_copy(hf_hbm, t, xs_hbm, p2_ref[k], sem).wait()
        return carry

    lax.fori_loop(0, tb, start, 0)
    lax.fori_loop(0, tb, wait, 0)


def _scatter_call(hf, p1, p2, zero_tiles, rows, tile, tb):
    n, d = hf.shape
    return pl.pallas_call(
        _scatter_kernel,
        grid_spec=pltpu.PrefetchScalarGridSpec(
            num_scalar_prefetch=1,
            grid=(n // tb,),
            in_specs=[
                pl.BlockSpec((tb,), lambda i, zt: (i,), memory_space=pltpu.SMEM),
                pl.BlockSpec((tb,), lambda i, zt: (i,), memory_space=pltpu.SMEM),
                pl.BlockSpec(memory_space=pl.ANY),
            ],
            out_specs=pl.BlockSpec(memory_space=pl.ANY),
            scratch_shapes=[pltpu.VMEM((tile, d), F32), pltpu.SemaphoreType.DMA(())],
        ),
        out_shape=jax.ShapeDtypeStruct((rows, d), F32),
        compiler_params=_params("arbitrary"),
        name="moe_scatter",
    )(zero_tiles, p1, p2, hf)


def _expert_kernel(te_ref, xs_ref, wg_ref, wu_ref, wd_ref, ys_ref):
    h = xs_ref[...].astype(BF16)
    half = wg_ref.shape[-1] // 2
    ys_ref[...] = (_swiglu_cols(h, wg_ref, wu_ref, wd_ref, 0, half)
                   + _swiglu_cols(h, wg_ref, wu_ref, wd_ref, half, 2 * half))


def _expert_call(xs, wg, wu, wd, tile_expert, tile):
    rows, d = xs.shape
    dff = wg.shape[-1]
    return pl.pallas_call(
        _expert_kernel,
        grid_spec=pltpu.PrefetchScalarGridSpec(
            num_scalar_prefetch=1,
            grid=(rows // tile,),
            in_specs=[
                pl.BlockSpec((tile, d), lambda r, te: (r, 0)),
                pl.BlockSpec((1, d, dff), lambda r, te: (te[r], 0, 0)),
                pl.BlockSpec((1, d, dff), lambda r, te: (te[r], 0, 0)),
                pl.BlockSpec((1, dff, d), lambda r, te: (te[r], 0, 0)),
            ],
            out_specs=pl.BlockSpec((tile, d), lambda r, te: (r, 0)),
        ),
        out_shape=jax.ShapeDtypeStruct((rows, d), F32),
        compiler_params=_params("arbitrary"),
        name="moe_experts",
    )(tile_expert, xs, wg, wu, wd)


def _combine_kernel(*refs, emit_h):
    if emit_h:
        (p1_ref, p2_ref, ys_hbm, rec_ref, x_ref, ada_ref, adan_ref, g_ref, b_ref,
         xo_ref, ho_ref, buf_ref, sem) = refs
    else:
        p1_ref, p2_ref, ys_hbm, rec_ref, x_ref, ada_ref, g_ref, b_ref, xo_ref, buf_ref, sem = refs
        adan_ref = ho_ref = None
    tm = x_ref.shape[0]

    def start(k, carry):
        _row_copy(ys_hbm, p1_ref[k], buf_ref.at[0], k, sem).start()
        _row_copy(ys_hbm, p2_ref[k], buf_ref.at[1], k, sem).start()
        return carry

    def wait(k, carry):
        _row_copy(ys_hbm, p1_ref[k], buf_ref.at[0], k, sem).wait()
        _row_copy(ys_hbm, p2_ref[k], buf_ref.at[1], k, sem).wait()
        return carry

    lax.fori_loop(0, tm, start, 0)
    lax.fori_loop(0, tm, wait, 0)
    w1 = rec_ref[:, R_W1:R_W1 + 1]
    w2 = rec_ref[:, R_W2:R_W2 + 1]
    ff = w1 * buf_ref[0] + w2 * buf_ref[1]
    _residual_epilogue(ff, x_ref, ada_ref, adan_ref, g_ref, b_ref, xo_ref, ho_ref)


def _combine_call(ys, p1, p2, rec, xf, ada_l, ada_next, ln_g, ln_b, seq, tm):
    n, d = xf.shape
    per = seq // tm
    emit_h = ada_next is not None
    row = lambda i: (i, 0)
    const = lambda i: (0, 0)
    ada_map = lambda i: (i // per, 0, 0)
    in_specs = [
        pl.BlockSpec((tm,), lambda i: (i,), memory_space=pltpu.SMEM),
        pl.BlockSpec((tm,), lambda i: (i,), memory_space=pltpu.SMEM),
        pl.BlockSpec(memory_space=pl.ANY),
        pl.BlockSpec((tm, LANES), row),
        pl.BlockSpec((tm, d), row),
        pl.BlockSpec((1, N_ADA, d), ada_map),
    ]
    args = [p1, p2, ys, rec, xf, ada_l]
    if emit_h:
        in_specs.append(pl.BlockSpec((1, N_ADA, d), ada_map))
        args.append(ada_next)
    in_specs += [pl.BlockSpec((1, d), const), pl.BlockSpec((1, d), const)]
    args += [ln_g.reshape(1, d), ln_b.reshape(1, d)]
    out_specs = [pl.BlockSpec((tm, d), row)]
    out_shape = [jax.ShapeDtypeStruct((n, d), F32)]
    if emit_h:
        out_specs.append(pl.BlockSpec((tm, d), row))
        out_shape.append(jax.ShapeDtypeStruct((n, d), BF16))
    outs = pl.pallas_call(
        functools.partial(_combine_kernel, emit_h=emit_h),
        grid=(n // tm,),
        in_specs=in_specs,
        out_specs=out_specs,
        out_shape=out_shape,
        scratch_shapes=[pltpu.VMEM((2, tm, d), F32), pltpu.SemaphoreType.DMA(())],
        compiler_params=_params("arbitrary"),
        name="moe_combine",
    )(*args)
    return outs if emit_h else (outs[0], None)


def _moe_call(hf, rec, counts, wg, wu, wd, xf, ada_l, ada_next, ln_g, ln_b, seq, tile, tm):
    n = hf.shape[0]
    total_tiles = 2 * n // tile + N_EXPERTS
    cnt = counts[0, :N_EXPERTS].astype(jnp.int32)
    ntiles = (cnt + tile - 1) // tile
    end_tile = jnp.cumsum(ntiles)
    start_row = (end_tile - ntiles) * tile
    p1 = start_row[rec[:, R_E1].astype(jnp.int32)] + rec[:, R_RANK1].astype(jnp.int32)
    p2 = start_row[rec[:, R_E2].astype(jnp.int32)] + rec[:, R_RANK2].astype(jnp.int32)
    tile_ids = jnp.arange(total_tiles, dtype=jnp.int32)
    tile_expert = jnp.minimum(jnp.sum(tile_ids[:, None] >= end_tile[None, :], axis=1), N_EXPERTS - 1)
    partial_last = jnp.where(cnt % tile != 0, end_tile - 1, -1)
    tail = end_tile[-1] + jnp.arange(N_EXPERTS, dtype=jnp.int32)
    zero_tiles = jnp.concatenate([partial_last, jnp.where(tail < total_tiles, tail, -1)]).astype(jnp.int32)

    xs = _scatter_call(hf, p1, p2, zero_tiles, total_tiles * tile, tile, tm)
    ys = _expert_call(xs, wg, wu, wd, tile_expert.astype(jnp.int32), tile)
    return _combine_call(ys, p1, p2, rec, xf, ada_l, ada_next, ln_g, ln_b, seq, tm)


def kernel(x, c, positions, w_ada, b_ada, w_in, w_dw, b_dw, ln_conv_g, ln_conv_b, w_conv_o, w_ret_o, w_out, ln1_g, ln1_b, ffn_w_gate, ffn_w_up, ffn_w_down, moe_w_router, moe_w_gate, moe_w_up, moe_w_down, ln2_g, ln2_b):
    bsz, seq, d = x.shape
    n = bsz * seq
    tm = 512
    tm_mix = 256
    fc = D_FF // 2
    tt = 256
    ret_chunk = 256
    moe_tile = 256

    xf = x.reshape(n, d)
    ada = _ada_call(c, w_ada, b_ada)
    cos, sin = _rope_call(positions, tm)
    h = _modulate_call(xf, ada[0], seq, tm)

    for l in range(DEPTH):
        w_in_b = w_in[l].astype(BF16)
        u = _glu_call(h, w_in_b, tm)
        qk = _qk_call(h, w_in_b, cos, sin, tm)
        v = _act_call(h, w_in_b, 4, 2, None, BF16, tm, "proj_v")
        ret_gate = _act_call(h, w_in_b, 6, 2, "silu", F32, tm, "proj_ret_gate")
        gates = _act_call(h, w_in_b, 8, 2, "sigmoid", F32, tm, "proj_gates")

        uc = _conv_call(u.reshape(bsz, seq, d), w_dw[l], b_dw[l], ln_conv_g[l], ln_conv_b[l], tt)
        og = _ret_call(qk, v, ret_gate, bsz, seq, ret_chunk)

        i = l // 2
        is_moe = l % 2 == 1
        mixed = _mix_call(
            uc.reshape(n, d), og, gates, xf, ada[l],
            w_conv_o[l].astype(BF16), w_ret_o[l].astype(BF16), w_out[l].astype(BF16),
            ln1_g[l], ln1_b[l], moe_w_router[i] if is_moe else None, seq, tm_mix)
        ada_next = ada[l + 1] if l + 1 < DEPTH else None
        if is_moe:
            xf, hf, rec, counts = mixed
            xf, h = _moe_call(hf, rec, counts, moe_w_gate[i].astype(BF16), moe_w_up[i].astype(BF16),
                              moe_w_down[i].astype(BF16), xf, ada[l], ada_next,
                              ln2_g[l], ln2_b[l], seq, moe_tile, tm)
        else:
            xf, h2 = mixed
            xf, h = _ffn_call(h2, ffn_w_gate[i][None].astype(BF16), ffn_w_up[i][None].astype(BF16),
                              ffn_w_down[i][None].astype(BF16), xf, ada[l], ada_next,
                              ln2_g[l], ln2_b[l], seq, tm, fc)
    return xf.reshape(bsz, seq, d)
```

```python
import functools
import math

import jax
import jax.numpy as jnp
from jax import lax
from jax.experimental import pallas as pl
from jax.experimental.pallas import tpu as pltpu

D_MODEL = 1024
DEPTH = 4
CONV_KERNEL = 31
RET_QK_DIM = 256
RET_V_DIM = 512
RET_HEADS = 4
RET_V_WIDTH = RET_HEADS * RET_V_DIM
RET_CHUNK = 128
ROPE_BASE = 10000.0
D_FF = 2816
N_EXPERTS = 8
N_ADA = 6
DEEPNORM_ALPHA = (2 * DEPTH) ** 0.25
LN_EPS = 1e-5

LANES = 128
SUBLANES = 8
CONV_HALO = 32
CONV_ROWS = 128
NORM_ROWS = 32
VMEM_LIMIT = 56 * 1024 * 1024

R_E1, R_E2, R_RANK1, R_RANK2, R_W1, R_W2 = range(6)

F32 = jnp.float32
BF16 = jnp.bfloat16


def _params(*sem):
    return pltpu.CompilerParams(dimension_semantics=sem, vmem_limit_bytes=VMEM_LIMIT)


def _bdot(a, b):
    return jnp.dot(a, b, preferred_element_type=F32)


def _split_bf16(a):
    hi = a.astype(BF16)
    lo = (a - hi.astype(F32)).astype(BF16)
    return hi, lo


def _dot3(a, b):
    a_hi, a_lo = _split_bf16(a)
    b_hi, b_lo = _split_bf16(b)
    return _bdot(a_hi, b_hi) + (_bdot(a_hi, b_lo) + _bdot(a_lo, b_hi))


def _sigmoid(x):
    return jax.nn.sigmoid(x)


def _layer_norm_rows(y, g, b):
    mu = jnp.mean(y, axis=-1, keepdims=True)
    d = y - mu
    var = jnp.mean(d * d, axis=-1, keepdims=True)
    return d * lax.rsqrt(var + LN_EPS) * g + b


def _ada_kernel(c_ref, w_ref, b_ref, o_ref):
    c = c_ref[...]
    o_ref[0] = _dot3(c * _sigmoid(c), w_ref[0]) + b_ref[0]


def _ada_call(c, w_ada, b_ada):
    depth, d, width = w_ada.shape
    bsz = c.shape[0]
    ncol = width // d
    out = pl.pallas_call(
        _ada_kernel,
        grid=(depth, ncol),
        in_specs=[
            pl.BlockSpec((bsz, d), lambda l, j: (0, 0)),
            pl.BlockSpec((1, d, d), lambda l, j: (l, 0, j)),
            pl.BlockSpec((1, 1, d), lambda l, j: (l, 0, j)),
        ],
        out_specs=pl.BlockSpec((1, bsz, d), lambda l, j: (l, 0, j)),
        out_shape=jax.ShapeDtypeStruct((depth, bsz, width), F32),
        compiler_params=_params("parallel", "parallel"),
        name="ada",
    )(c, w_ada, b_ada.reshape(depth, 1, width))
    return out.reshape(depth, bsz, N_ADA, d)


def _rope_kernel(pos_ref, inv_ref, cos_ref, sin_ref):
    ang = pos_ref[...].astype(F32) * inv_ref[...]
    cos_ref[...] = jnp.cos(ang)
    sin_ref[...] = jnp.sin(ang)


def _rope_call(positions, tm):
    n = positions.size
    half = RET_QK_DIM // 2
    inv_freq = ROPE_BASE ** (-jnp.arange(half, dtype=F32) / half)
    return pl.pallas_call(
        _rope_kernel,
        grid=(n // tm,),
        in_specs=[
            pl.BlockSpec((tm, 1), lambda i: (i, 0)),
            pl.BlockSpec((1, half), lambda i: (0, 0)),
        ],
        out_specs=[pl.BlockSpec((tm, half), lambda i: (i, 0))] * 2,
        out_shape=[jax.ShapeDtypeStruct((n, half), F32)] * 2,
        compiler_params=_params("parallel"),
        name="rope",
    )(positions.reshape(n, 1), inv_freq.reshape(1, half))


def _modulate_kernel(x_ref, ada_ref, h_ref):
    shift = ada_ref[0, 0:1, :]
    scale = ada_ref[0, 1:2, :]
    h_ref[...] = (x_ref[...] * (1.0 + scale) + shift).astype(BF16)


def _modulate_call(xf, ada_l, seq, tm):
    n, d = xf.shape
    per = seq // tm
    return pl.pallas_call(
        _modulate_kernel,
        grid=(n // tm,),
        in_specs=[
            pl.BlockSpec((tm, d), lambda i: (i, 0)),
            pl.BlockSpec((1, N_ADA, d), lambda i: (i // per, 0, 0)),
        ],
        out_specs=pl.BlockSpec((tm, d), lambda i: (i, 0)),
        out_shape=jax.ShapeDtypeStruct((n, d), BF16),
        compiler_params=_params("parallel"),
        name="modulate",
    )(xf, ada_l)


def _glu_kernel(h_ref, wa_ref, wb_ref, u_ref):
    h = h_ref[...]
    a = _bdot(h, wa_ref[...])
    b = _bdot(h, wb_ref[...])
    u_ref[...] = a * _sigmoid(b)


def _glu_call(h, w_in_b, tm):
    n, d = h.shape
    return pl.pallas_call(
        _glu_kernel,
        grid=(n // tm,),
        in_specs=[
            pl.BlockSpec((tm, d), lambda i: (i, 0)),
            pl.BlockSpec((d, d), lambda i: (0, 0)),
            pl.BlockSpec((d, d), lambda i: (0, 1)),
        ],
        out_specs=pl.BlockSpec((tm, d), lambda i: (i, 0)),
        out_shape=jax.ShapeDtypeStruct((n, d), F32),
        compiler_params=_params("parallel"),
        name="proj_glu",
    )(h, w_in_b, w_in_b)


def _qk_kernel(h_ref, w_ref, cos_ref, sin_ref, o_ref):
    acc = _bdot(h_ref[...], w_ref[...])
    scale = jnp.where(pl.program_id(0) == 0, 1.0, RET_QK_DIM ** -0.5).astype(F32)
    cos = cos_ref[...] * scale
    sin = sin_ref[...] * scale
    half = RET_QK_DIM // 2
    for hd in range(RET_HEADS):
        lo = hd * RET_QK_DIM
        x1 = acc[:, lo:lo + half]
        x2 = acc[:, lo + half:lo + RET_QK_DIM]
        o_ref[0, :, lo:lo + half] = (x1 * cos - x2 * sin).astype(BF16)
        o_ref[0, :, lo + half:lo + RET_QK_DIM] = (x2 * cos + x1 * sin).astype(BF16)


def _qk_call(h, w_in_b, cos, sin, tm):
    n, d = h.shape
    half = RET_QK_DIM // 2
    return pl.pallas_call(
        _qk_kernel,
        grid=(2, n // tm),
        in_specs=[
            pl.BlockSpec((tm, d), lambda j, i: (i, 0)),
            pl.BlockSpec((d, d), lambda j, i: (0, 2 + j)),
            pl.BlockSpec((tm, half), lambda j, i: (i, 0)),
            pl.BlockSpec((tm, half), lambda j, i: (i, 0)),
        ],
        out_specs=pl.BlockSpec((1, tm, d), lambda j, i: (j, i, 0)),
        out_shape=jax.ShapeDtypeStruct((2, n, d), BF16),
        compiler_params=_params("parallel", "parallel"),
        name="proj_qk",
    )(h, w_in_b, cos, sin)


def _vgg_kernel(h_ref, w_ref, o_ref):
    j = pl.program_id(0)
    acc = _bdot(h_ref[...], w_ref[...])

    @pl.when(j < 2)
    def _():
        o_ref[...] = acc.astype(BF16)

    @pl.when(jnp.logical_and(j >= 2, j < 4))
    def _():
        o_ref[...] = (acc * _sigmoid(acc)).astype(BF16)

    @pl.when(j >= 4)
    def _():
        o_ref[...] = _sigmoid(acc).astype(BF16)


def _vgg_call(h, w_in_b, tm):
    n, d = h.shape
    col0, ncols = 4, 6
    return pl.pallas_call(
        _vgg_kernel,
        grid=(ncols, n // tm),
        in_specs=[
            pl.BlockSpec((tm, d), lambda j, i: (i, 0)),
            pl.BlockSpec((d, d), lambda j, i: (0, col0 + j)),
        ],
        out_specs=pl.BlockSpec((tm, d), lambda j, i: (i, j)),
        out_shape=jax.ShapeDtypeStruct((n, ncols * d), BF16),
        compiler_params=_params("parallel", "parallel"),
        name="proj_v_gates",
    )(h, w_in_b)


def _conv_kernel(cur_ref, prev_ref, w_ref, bias_ref, g_ref, b_ref, o_ref, buf_ref, y_ref, *, tt):
    ch = cur_ref.shape[-1]
    first = pl.program_id(1) == 0
    buf_ref[0, 0:CONV_HALO, :] = jnp.where(first, 0.0, prev_ref[0])
    buf_ref[0, CONV_HALO:, :] = cur_ref[0]

    sub = lax.broadcasted_iota(jnp.int32, (SUBLANES, ch), 0)
    n_shift_tiles = (tt + CONV_HALO) // SUBLANES - 1
    for s in range(1, SUBLANES):
        own = sub < SUBLANES - s
        cur = pltpu.roll(buf_ref[0, 0:SUBLANES, :], SUBLANES - s, 0)
        for j in range(n_shift_tiles):
            nxt = pltpu.roll(buf_ref[0, (j + 1) * SUBLANES:(j + 2) * SUBLANES, :], SUBLANES - s, 0)
            buf_ref[s, j * SUBLANES:(j + 1) * SUBLANES, :] = jnp.where(own, cur, nxt)
            cur = nxt

    base = CONV_HALO - (CONV_KERNEL - 1)
    for lg in range(ch // LANES):
        ls = slice(lg * LANES, (lg + 1) * LANES)
        bias = bias_ref[:, ls]
        for r0 in range(0, tt, CONV_ROWS):
            acc = jnp.zeros((CONV_ROWS, LANES), F32)
            for k in range(CONV_KERNEL):
                a, s = divmod(base + k, SUBLANES)
                lo = r0 + a * SUBLANES
                acc = acc + buf_ref[s, lo:lo + CONV_ROWS, ls] * w_ref[k:k + 1, ls]
            y_ref[r0:r0 + CONV_ROWS, ls] = acc + bias

    g = g_ref[...]
    b = b_ref[...]
    for r0 in range(0, tt, NORM_ROWS):
        y = _layer_norm_rows(y_ref[r0:r0 + NORM_ROWS, :], g, b)
        o_ref[0, r0:r0 + NORM_ROWS, :] = (y * _sigmoid(y)).astype(BF16)


def _conv_call(u, w_dw, b_dw, ln_g, ln_b, tt):
    bsz, seq, ch = u.shape
    per = tt // CONV_HALO
    return pl.pallas_call(
        functools.partial(_conv_kernel, tt=tt),
        grid=(bsz, seq // tt),
        in_specs=[
            pl.BlockSpec((1, tt, ch), lambda b, t: (b, t, 0)),
            pl.BlockSpec((1, CONV_HALO, ch), lambda b, t: (b, jnp.maximum(t * per - 1, 0), 0)),
            pl.BlockSpec((CONV_KERNEL, ch), lambda b, t: (0, 0)),
            pl.BlockSpec((1, ch), lambda b, t: (0, 0)),
            pl.BlockSpec((1, ch), lambda b, t: (0, 0)),
            pl.BlockSpec((1, ch), lambda b, t: (0, 0)),
        ],
        out_specs=pl.BlockSpec((1, tt, ch), lambda b, t: (b, t, 0)),
        out_shape=jax.ShapeDtypeStruct((bsz, seq, ch), BF16),
        scratch_shapes=[pltpu.VMEM((SUBLANES, CONV_HALO + tt, ch), F32), pltpu.VMEM((tt, ch), F32)],
        compiler_params=_params("parallel", "parallel"),
        name="conv_ln",
    )(u, u, w_dw, b_dw.reshape(1, ch), ln_g.reshape(1, ch), ln_b.reshape(1, ch))


def _ret_kernel(q_ref, k_ref, v_ref, g_ref, o_ref, s_ref, *, chunk):
    @pl.when(pl.program_id(1) == 0)
    def _():
        s_ref[...] = jnp.zeros_like(s_ref)

    row = lax.broadcasted_iota(jnp.int32, (chunk, 1), 0).astype(F32)
    diff = (lax.broadcasted_iota(jnp.int32, (chunk, chunk), 0)
            - lax.broadcasted_iota(jnp.int32, (chunk, chunk), 1)).astype(F32)
    causal = diff >= 0
    lag = jnp.maximum(diff, 0.0)
    for hd in range(RET_HEADS):
        log_g = math.log1p(-(2.0 ** (-5 - hd)))
        q_decay = jnp.exp(log_g * (row + 1.0))
        k_decay = jnp.exp(log_g * (chunk - 1.0 - row))
        chunk_decay = math.exp(log_g * chunk)
        intra = jnp.where(causal, jnp.exp(log_g * lag), 0.0)

        qs = slice(hd * RET_QK_DIM, (hd + 1) * RET_QK_DIM)
        vs = slice(hd * RET_V_DIM, (hd + 1) * RET_V_DIM)
        q = q_ref[:, qs]
        k = k_ref[:, qs]
        v = v_ref[:, vs]
        state = s_ref[hd]
        scores = lax.dot_general(q, k, (((1,), (1,)), ((), ())), preferred_element_type=F32)
        inner = _bdot((scores * intra).astype(BF16), v)
        cross = _bdot(q, state.astype(BF16)) * q_decay
        update = lax.dot_general((k.astype(F32) * k_decay).astype(BF16), v, (((0,), (0,)), ((), ())),
                                 preferred_element_type=F32)
        s_ref[hd] = state * chunk_decay + update

        o = inner + cross
        mu = jnp.mean(o, axis=-1, keepdims=True)
        d = o - mu
        var = jnp.mean(d * d, axis=-1, keepdims=True)
        o_ref[:, vs] = (g_ref[:, vs] * (d * lax.rsqrt(var + LN_EPS))).astype(BF16)


def _ret_call(qk, vgg, bsz, seq, chunk):
    width = RET_HEADS * RET_QK_DIM
    qk4 = qk.reshape(2, bsz, seq, width)
    v3 = g3 = vgg.reshape(bsz, seq, vgg.shape[-1])
    out = pl.pallas_call(
        functools.partial(_ret_kernel, chunk=chunk),
        grid=(bsz, seq // chunk),
        in_specs=[
            pl.BlockSpec((None, None, chunk, width), lambda b, n: (0, b, n, 0)),
            pl.BlockSpec((None, None, chunk, width), lambda b, n: (1, b, n, 0)),
            pl.BlockSpec((None, chunk, RET_V_WIDTH), lambda b, n: (b, n, 0)),
            pl.BlockSpec((None, chunk, RET_V_WIDTH), lambda b, n: (b, n, 1)),
        ],
        out_specs=pl.BlockSpec((None, chunk, RET_V_WIDTH), lambda b, n: (b, n, 0)),
        out_shape=jax.ShapeDtypeStruct((bsz, seq, RET_V_WIDTH), BF16),
        scratch_shapes=[pltpu.VMEM((RET_HEADS, RET_QK_DIM, RET_V_DIM), F32)],
        compiler_params=_params("parallel", "arbitrary"),
        name="retention",
    )(qk4, qk4, v3, g3)
    return out.reshape(bsz * seq, RET_V_WIDTH)


def _route(logits, carry_ref):
    rows = logits.shape[0]
    lane = lax.broadcasted_iota(jnp.int32, logits.shape, 1)
    neg = jnp.float32(-jnp.inf)
    lg = jnp.where(lane < N_EXPERTS, logits, neg)
    m1 = jnp.max(lg, axis=-1, keepdims=True)
    i1 = jnp.min(jnp.where(lg == m1, lane, LANES), axis=-1, keepdims=True)
    lg2 = jnp.where(lane == i1, neg, lg)
    m2 = jnp.max(lg2, axis=-1, keepdims=True)
    i2 = jnp.min(jnp.where(lg2 == m2, lane, LANES), axis=-1, keepdims=True)
    e = jnp.exp(m2 - m1)
    w1 = 1.0 / (1.0 + e)
    w2 = e / (1.0 + e)

    sel = jnp.where(jnp.logical_or(lane == i1, lane == i2), 1.0, 0.0)
    tri = jnp.where(lax.broadcasted_iota(jnp.int32, (rows, rows), 0)
                    >= lax.broadcasted_iota(jnp.int32, (rows, rows), 1), 1.0, 0.0).astype(BF16)
    incl = _bdot(tri, sel.astype(BF16))
    rank = carry_ref[...] + incl - sel
    carry_ref[...] = carry_ref[...] + incl[rows - 1:rows, :]
    r1 = jnp.sum(jnp.where(lane == i1, rank, 0.0), axis=-1, keepdims=True)
    r2 = jnp.sum(jnp.where(lane == i2, rank, 0.0), axis=-1, keepdims=True)

    rec = jnp.zeros(logits.shape, F32)
    for idx, val in ((R_E1, i1.astype(F32)), (R_E2, i2.astype(F32)), (R_RANK1, r1), (R_RANK2, r2),
                     (R_W1, w1), (R_W2, w2)):
        rec = jnp.where(lane == idx, val, rec)
    return rec


def _mix_kernel(*refs, router):
    if router:
        (uc_ref, og_ref, gt_ref, x_ref, ada_ref, wc_ref, wr_ref, wo_ref, g_ref, b_ref, wrt_ref,
         xo_ref, ho_ref, rec_ref, cnt_ref, carry_ref) = refs
    else:
        (uc_ref, og_ref, gt_ref, x_ref, ada_ref, wc_ref, wr_ref, wo_ref, g_ref, b_ref,
         xo_ref, ho_ref) = refs
    d = x_ref.shape[-1]
    y_conv = _bdot(uc_ref[...], wc_ref[...])
    y_ret = _bdot(og_ref[...], wr_ref[...])
    merged = gt_ref[:, 0:d] * y_conv + gt_ref[:, d:2 * d] * y_ret
    mix = _bdot(merged.astype(BF16), wo_ref[...])
    gate1 = ada_ref[0, 2:3, :]
    shift2 = ada_ref[0, 3:4, :]
    scale2 = ada_ref[0, 4:5, :]
    xn = _layer_norm_rows(DEEPNORM_ALPHA * x_ref[...] + gate1 * mix, g_ref[...], b_ref[...])
    xo_ref[...] = xn
    h2 = xn * (1.0 + scale2) + shift2
    ho_ref[...] = h2.astype(ho_ref.dtype)
    if router:
        @pl.when(pl.program_id(0) == 0)
        def _():
            carry_ref[...] = jnp.zeros_like(carry_ref)

        rec_ref[...] = _route(_dot3(h2, wrt_ref[...]), carry_ref)
        cnt_ref[...] = carry_ref[...]


def _mix_call(uc, og, gates, xf, ada_l, wc, wr, wo, ln_g, ln_b, w_router, seq, tm):
    n, d = xf.shape
    per = seq // tm
    router = w_router is not None
    row = lambda i: (i, 0)
    const = lambda i: (0, 0)
    in_specs = [
        pl.BlockSpec((tm, d), row),
        pl.BlockSpec((tm, RET_V_WIDTH), row),
        pl.BlockSpec((tm, 2 * d), lambda i: (i, 2)),
        pl.BlockSpec((tm, d), row),
        pl.BlockSpec((1, N_ADA, d), lambda i: (i // per, 0, 0)),
        pl.BlockSpec((d, d), const),
        pl.BlockSpec((RET_V_WIDTH, d), const),
        pl.BlockSpec((d, d), const),
        pl.BlockSpec((1, d), const),
        pl.BlockSpec((1, d), const),
    ]
    args = [uc, og, gates, xf, ada_l, wc, wr, wo, ln_g.reshape(1, d), ln_b.reshape(1, d)]
    out_specs = [pl.BlockSpec((tm, d), row), pl.BlockSpec((tm, d), row)]
    out_shape = [jax.ShapeDtypeStruct((n, d), F32), jax.ShapeDtypeStruct((n, d), F32 if router else BF16)]
    scratch = []
    if router:
        wrt = jnp.zeros((d, LANES), F32).at[:, :N_EXPERTS].set(w_router)
        in_specs.append(pl.BlockSpec((d, LANES), const))
        args.append(wrt)
        out_specs += [pl.BlockSpec((tm, LANES), row), pl.BlockSpec((1, LANES), const)]
        out_shape += [jax.ShapeDtypeStruct((n, LANES), F32), jax.ShapeDtypeStruct((1, LANES), F32)]
        scratch = [pltpu.VMEM((1, LANES), F32)]
    return pl.pallas_call(
        functools.partial(_mix_kernel, router=router),
        grid=(n // tm,),
        in_specs=in_specs,
        out_specs=out_specs,
        out_shape=out_shape,
        scratch_shapes=scratch,
        compiler_params=_params("arbitrary" if router else "parallel"),
        name="mix_out_router" if router else "mix_out",
    )(*args)


def _residual_epilogue(ff, x_ref, ada_ref, adan_ref, g_ref, b_ref, xo_ref, ho_ref):
    gate2 = ada_ref[0, 5:6, :]
    xn = _layer_norm_rows(DEEPNORM_ALPHA * x_ref[...] + gate2 * ff, g_ref[...], b_ref[...])
    xo_ref[...] = xn
    if ho_ref is not None:
        shift1 = adan_ref[0, 0:1, :]
        scale1 = adan_ref[0, 1:2, :]
        ho_ref[...] = (xn * (1.0 + scale1) + shift1).astype(BF16)


def _swiglu_cols(h, wg_ref, wu_ref, wd_ref, lo, hi):
    gate = _bdot(h, wg_ref[0, :, lo:hi])
    up = _bdot(h, wu_ref[0, :, lo:hi])
    return _bdot((gate * _sigmoid(gate) * up).astype(BF16), wd_ref[0, lo:hi, :])


def _ffn_kernel(*refs, emit_h):
    if emit_h:
        h_ref, wg_ref, wu_ref, wd_ref, x_ref, ada_ref, adan_ref, g_ref, b_ref, xo_ref, ho_ref, acc_ref = refs
    else:
        h_ref, wg_ref, wu_ref, wd_ref, x_ref, ada_ref, g_ref, b_ref, xo_ref, acc_ref = refs
        adan_ref = ho_ref = None
    c = pl.program_id(1)

    @pl.when(c == 0)
    def _():
        acc_ref[...] = jnp.zeros_like(acc_ref)

    acc_ref[...] += _swiglu_cols(h_ref[...], wg_ref, wu_ref, wd_ref, 0, wg_ref.shape[-1])

    @pl.when(c == pl.num_programs(1) - 1)
    def _():
        _residual_epilogue(acc_ref[...], x_ref, ada_ref, adan_ref, g_ref, b_ref, xo_ref, ho_ref)


def _ffn_call(h, wg, wu, wd, xf, ada_l, ada_next, ln_g, ln_b, seq, tm, fc):
    n, d = xf.shape
    dff = wg.shape[-1]
    per = seq // tm
    emit_h = ada_next is not None
    row = lambda i, c: (i, 0)
    const = lambda i, c: (0, 0)
    ada_map = lambda i, c: (i // per, 0, 0)
    in_specs = [
        pl.BlockSpec((tm, d), row),
        pl.BlockSpec((1, d, fc), lambda i, c: (0, 0, c)),
        pl.BlockSpec((1, d, fc), lambda i, c: (0, 0, c)),
        pl.BlockSpec((1, fc, d), lambda i, c: (0, c, 0)),
        pl.BlockSpec((tm, d), row),
        pl.BlockSpec((1, N_ADA, d), ada_map),
    ]
    args = [h, wg, wu, wd, xf, ada_l]
    if emit_h:
        in_specs.append(pl.BlockSpec((1, N_ADA, d), ada_map))
        args.append(ada_next)
    in_specs += [pl.BlockSpec((1, d), const), pl.BlockSpec((1, d), const)]
    args += [ln_g.reshape(1, d), ln_b.reshape(1, d)]
    out_specs = [pl.BlockSpec((tm, d), row)]
    out_shape = [jax.ShapeDtypeStruct((n, d), F32)]
    if emit_h:
        out_specs.append(pl.BlockSpec((tm, d), row))
        out_shape.append(jax.ShapeDtypeStruct((n, d), BF16))
    outs = pl.pallas_call(
        functools.partial(_ffn_kernel, emit_h=emit_h),
        grid=(n // tm, dff // fc),
        in_specs=in_specs,
        out_specs=out_specs,
        out_shape=out_shape,
        scratch_shapes=[pltpu.VMEM((tm, d), F32)],
        compiler_params=_params("parallel", "arbitrary"),
        name="dense_ffn",
    )(*args)
    return outs if emit_h else (outs[0], None)


def _row_copy(src, src_row, dst, dst_row, sem):
    return pltpu.make_async_copy(src.at[pl.ds(src_row, 1)], dst.at[pl.ds(dst_row, 1)], sem)


def _scatter_kernel(zt_ref, p1_ref, p2_ref, hf_ref, xs_hbm, zero_ref, sem):
    i = pl.program_id(0)
    tb = p1_ref.shape[0]
    tile = zero_ref.shape[0]

    def zero_copy(j):
        return pltpu.make_async_copy(zero_ref, xs_hbm.at[pl.ds(zt_ref[j] * tile, tile)], sem)

    @pl.when(i == 0)
    def _():
        zero_ref[...] = jnp.zeros_like(zero_ref)
        for j in range(zt_ref.shape[0]):
            pl.when(zt_ref[j] >= 0)(lambda j=j: zero_copy(j).start())
        for j in range(zt_ref.shape[0]):
            pl.when(zt_ref[j] >= 0)(lambda j=j: zero_copy(j).wait())

    def start(k, carry):
        _row_copy(hf_ref, k, xs_hbm, p1_ref[k], sem).start()
        _row_copy(hf_ref, k, xs_hbm, p2_ref[k], sem).start()
        return carry

    def wait(k, carry):
        _row_copy(hf_ref, k, xs_hbm, p1_ref[k], sem).wait()
        _row_copy(hf_ref, k, xs_hbm, p2_ref[k], sem).wait()
        return carry

    lax.fori_loop(0, tb, start, 0, unroll=8)
    lax.fori_loop(0, tb, wait, 0, unroll=8)


def _scatter_call(hf, p1, p2, zero_tiles, rows, tile, tb):
    n, d = hf.shape
    return pl.pallas_call(
        _scatter_kernel,
        grid_spec=pltpu.PrefetchScalarGridSpec(
            num_scalar_prefetch=1,
            grid=(n // tb,),
            in_specs=[
                pl.BlockSpec((tb,), lambda i, zt: (i,), memory_space=pltpu.SMEM),
                pl.BlockSpec((tb,), lambda i, zt: (i,), memory_space=pltpu.SMEM),
                pl.BlockSpec((tb, d), lambda i, zt: (i, 0)),
            ],
            out_specs=pl.BlockSpec(memory_space=pl.ANY),
            scratch_shapes=[pltpu.VMEM((tile, d), F32), pltpu.SemaphoreType.DMA(())],
        ),
        out_shape=jax.ShapeDtypeStruct((rows, d), F32),
        compiler_params=_params("arbitrary"),
        name="moe_scatter",
    )(zero_tiles, p1, p2, hf)


def _expert_kernel(te_ref, xs_ref, wg_ref, wu_ref, wd_ref, ys_ref):
    h = xs_ref[...].astype(BF16)
    half = wg_ref.shape[-1] // 2
    ys_ref[...] = (_swiglu_cols(h, wg_ref, wu_ref, wd_ref, 0, half)
                   + _swiglu_cols(h, wg_ref, wu_ref, wd_ref, half, 2 * half))


def _expert_call(xs, wg, wu, wd, tile_expert, tile):
    rows, d = xs.shape
    dff = wg.shape[-1]
    return pl.pallas_call(
        _expert_kernel,
        grid_spec=pltpu.PrefetchScalarGridSpec(
            num_scalar_prefetch=1,
            grid=(rows // tile,),
            in_specs=[
                pl.BlockSpec((tile, d), lambda r, te: (r, 0)),
                pl.BlockSpec((1, d, dff), lambda r, te: (te[r], 0, 0)),
                pl.BlockSpec((1, d, dff), lambda r, te: (te[r], 0, 0)),
                pl.BlockSpec((1, dff, d), lambda r, te: (te[r], 0, 0)),
            ],
            out_specs=pl.BlockSpec((tile, d), lambda r, te: (r, 0)),
        ),
        out_shape=jax.ShapeDtypeStruct((rows, d), F32),
        compiler_params=_params("arbitrary"),
        name="moe_experts",
    )(tile_expert, xs, wg, wu, wd)


def _combine_kernel(*refs, emit_h):
    if emit_h:
        (p1_ref, p2_ref, ys_hbm, rec_ref, x_ref, ada_ref, adan_ref, g_ref, b_ref,
         xo_ref, ho_ref, buf_ref, sem) = refs
    else:
        p1_ref, p2_ref, ys_hbm, rec_ref, x_ref, ada_ref, g_ref, b_ref, xo_ref, buf_ref, sem = refs
        adan_ref = ho_ref = None
    tm = x_ref.shape[0]

    def start(k, carry):
        _row_copy(ys_hbm, p1_ref[k], buf_ref.at[0], k, sem).start()
        _row_copy(ys_hbm, p2_ref[k], buf_ref.at[1], k, sem).start()
        return carry

    def wait(k, carry):
        _row_copy(ys_hbm, p1_ref[k], buf_ref.at[0], k, sem).wait()
        _row_copy(ys_hbm, p2_ref[k], buf_ref.at[1], k, sem).wait()
        return carry

    lax.fori_loop(0, tm, start, 0, unroll=8)
    lax.fori_loop(0, tm, wait, 0, unroll=8)
    w1 = rec_ref[:, R_W1:R_W1 + 1]
    w2 = rec_ref[:, R_W2:R_W2 + 1]
    ff = w1 * buf_ref[0] + w2 * buf_ref[1]
    _residual_epilogue(ff, x_ref, ada_ref, adan_ref, g_ref, b_ref, xo_ref, ho_ref)


def _combine_call(ys, p1, p2, rec, xf, ada_l, ada_next, ln_g, ln_b, seq, tm):
    n, d = xf.shape
    per = seq // tm
    emit_h = ada_next is not None
    row = lambda i: (i, 0)
    const = lambda i: (0, 0)
    ada_map = lambda i: (i // per, 0, 0)
    in_specs = [
        pl.BlockSpec((tm,), lambda i: (i,), memory_space=pltpu.SMEM),
        pl.BlockSpec((tm,), lambda i: (i,), memory_space=pltpu.SMEM),
        pl.BlockSpec(memory_space=pl.ANY),
        pl.BlockSpec((tm, LANES), row),
        pl.BlockSpec((tm, d), row),
        pl.BlockSpec((1, N_ADA, d), ada_map),
    ]
    args = [p1, p2, ys, rec, xf, ada_l]
    if emit_h:
        in_specs.append(pl.BlockSpec((1, N_ADA, d), ada_map))
        args.append(ada_next)
    in_specs += [pl.BlockSpec((1, d), const), pl.BlockSpec((1, d), const)]
    args += [ln_g.reshape(1, d), ln_b.reshape(1, d)]
    out_specs = [pl.BlockSpec((tm, d), row)]
    out_shape = [jax.ShapeDtypeStruct((n, d), F32)]
    if emit_h:
        out_specs.append(pl.BlockSpec((tm, d), row))
        out_shape.append(jax.ShapeDtypeStruct((n, d), BF16))
    outs = pl.pallas_call(
        functools.partial(_combine_kernel, emit_h=emit_h),
        grid=(n // tm,),
        in_specs=in_specs,
        out_specs=out_specs,
        out_shape=out_shape,
        scratch_shapes=[pltpu.VMEM((2, tm, d), F32), pltpu.SemaphoreType.DMA(())],
        compiler_params=_params("arbitrary"),
        name="moe_combine",
    )(*args)
    return outs if emit_h else (outs[0], None)


def _moe_call(hf, rec, counts, wg, wu, wd, xf, ada_l, ada_next, ln_g, ln_b, seq, tile, tm):
    n = hf.shape[0]
    total_tiles = 2 * n // tile + N_EXPERTS
    cnt = counts[0, :N_EXPERTS].astype(jnp.int32)
    ntiles = (cnt + tile - 1) // tile
    end_tile = jnp.cumsum(ntiles)
    start_row = (end_tile - ntiles) * tile
    p1 = start_row[rec[:, R_E1].astype(jnp.int32)] + rec[:, R_RANK1].astype(jnp.int32)
    p2 = start_row[rec[:, R_E2].astype(jnp.int32)] + rec[:, R_RANK2].astype(jnp.int32)
    tile_ids = jnp.arange(total_tiles, dtype=jnp.int32)
    tile_expert = jnp.minimum(jnp.sum(tile_ids[:, None] >= end_tile[None, :], axis=1), N_EXPERTS - 1)
    partial_last = jnp.where(cnt % tile != 0, end_tile - 1, -1)
    tail = end_tile[-1] + jnp.arange(N_EXPERTS, dtype=jnp.int32)
    zero_tiles = jnp.concatenate([partial_last, jnp.where(tail < total_tiles, tail, -1)]).astype(jnp.int32)

    xs = _scatter_call(hf, p1, p2, zero_tiles, total_tiles * tile, tile, tm)
    ys = _expert_call(xs, wg, wu, wd, tile_expert.astype(jnp.int32), tile)
    return _combine_call(ys, p1, p2, rec, xf, ada_l, ada_next, ln_g, ln_b, seq, tm)


def kernel(x, c, positions, w_ada, b_ada, w_in, w_dw, b_dw, ln_conv_g, ln_conv_b, w_conv_o, w_ret_o, w_out, ln1_g, ln1_b, ffn_w_gate, ffn_w_up, ffn_w_down, moe_w_router, moe_w_gate, moe_w_up, moe_w_down, ln2_g, ln2_b):
    bsz, seq, d = x.shape
    n = bsz * seq
    tm = 512
    tm_proj = 1024
    tm_mix = 256
    fc = D_FF // 2
    tt = 256
    ret_chunk = 256
    moe_tile = 256

    xf = x.reshape(n, d)
    ada = _ada_call(c, w_ada, b_ada)
    cos, sin = _rope_call(positions, tm)
    h = _modulate_call(xf, ada[0], seq, tm)

    for l in range(DEPTH):
        w_in_b = w_in[l].astype(BF16)
        u = _glu_call(h, w_in_b, tm_proj)
        qk = _qk_call(h, w_in_b, cos, sin, tm_proj)
        vgg = _vgg_call(h, w_in_b, tm_proj)

        uc = _conv_call(u.reshape(bsz, seq, d), w_dw[l], b_dw[l], ln_conv_g[l], ln_conv_b[l], tt)
        og = _ret_call(qk, vgg, bsz, seq, ret_chunk)

        i = l // 2
        is_moe = l % 2 == 1
        mixed = _mix_call(
            uc.reshape(n, d), og, vgg, xf, ada[l],
            w_conv_o[l].astype(BF16), w_ret_o[l].astype(BF16), w_out[l].astype(BF16),
            ln1_g[l], ln1_b[l], moe_w_router[i] if is_moe else None, seq, tm_mix)
        ada_next = ada[l + 1] if l + 1 < DEPTH else None
        if is_moe:
            xf, hf, rec, counts = mixed
            xf, h = _moe_call(hf, rec, counts, moe_w_gate[i].astype(BF16), moe_w_up[i].astype(BF16),
                              moe_w_down[i].astype(BF16), xf, ada[l], ada_next,
                              ln2_g[l], ln2_b[l], seq, moe_tile, tm)
        else:
            xf, h2 = mixed
            xf, h = _ffn_call(h2, ffn_w_gate[i][None].astype(BF16), ffn_w_up[i][None].astype(BF16),
                              ffn_w_down[i][None].astype(BF16), xf, ada[l], ada_next,
                              ln2_g[l], ln2_b[l], seq, tm, fc)
    return xf.reshape(bsz, seq, d)
```

```python
import functools
import math

import jax
import jax.numpy as jnp
from jax import lax
from jax.experimental import pallas as pl
from jax.experimental.pallas import tpu as pltpu

D_MODEL = 1024
DEPTH = 4
CONV_KERNEL = 31
RET_QK_DIM = 256
RET_V_DIM = 512
RET_HEADS = 4
RET_V_WIDTH = RET_HEADS * RET_V_DIM
RET_CHUNK = 128
ROPE_BASE = 10000.0
D_FF = 2816
N_EXPERTS = 8
N_ADA = 6
DEEPNORM_ALPHA = (2 * DEPTH) ** 0.25
LN_EPS = 1e-5

LANES = 128
SUBLANES = 8
CONV_HALO = 32
CONV_ROWS = 128
NORM_ROWS = 32
CAST_ROWS = 512
CAST_MAX_COLS = D_FF
VMEM_LIMIT = 56 * 1024 * 1024

R_E1, R_E2, R_RANK1, R_RANK2, R_W1, R_W2 = range(6)

F32 = jnp.float32
BF16 = jnp.bfloat16


def _params(*sem):
    return pltpu.CompilerParams(dimension_semantics=sem, vmem_limit_bytes=VMEM_LIMIT)


def _bdot(a, b):
    return jnp.dot(a, b, preferred_element_type=F32)


def _split_bf16(a):
    hi = a.astype(BF16)
    lo = (a - hi.astype(F32)).astype(BF16)
    return hi, lo


def _dot3(a, b):
    a_hi, a_lo = _split_bf16(a)
    b_hi, b_lo = _split_bf16(b)
    return _bdot(a_hi, b_hi) + (_bdot(a_hi, b_lo) + _bdot(a_lo, b_hi))


def _sigmoid(x):
    return jax.nn.sigmoid(x)


def _layer_norm_rows(y, g, b):
    mu = jnp.mean(y, axis=-1, keepdims=True)
    d = y - mu
    var = jnp.mean(d * d, axis=-1, keepdims=True)
    return d * lax.rsqrt(var + LN_EPS) * g + b


def _cast_kernel(w_ref, o_ref):
    o_ref[...] = w_ref[...].astype(BF16)


def _cast_call(w):
    cols = w.shape[-1]
    w2 = w.reshape(-1, cols)
    rows = w2.shape[0]
    bc = cols
    while bc > CAST_MAX_COLS:
        bc //= 2
    assert rows % CAST_ROWS == 0 and cols % bc == 0 and bc % LANES == 0, (w.shape, bc)
    out = pl.pallas_call(
        _cast_kernel,
        grid=(rows // CAST_ROWS, cols // bc),
        in_specs=[pl.BlockSpec((CAST_ROWS, bc), lambda i, j: (i, j))],
        out_specs=pl.BlockSpec((CAST_ROWS, bc), lambda i, j: (i, j)),
        out_shape=jax.ShapeDtypeStruct((rows, cols), BF16),
        compiler_params=_params("parallel", "parallel"),
        name="cast_bf16",
    )(w2)
    return out.reshape(w.shape)


def _ada_kernel(c_ref, w_ref, b_ref, o_ref):
    c = c_ref[...]
    o_ref[0] = _dot3(c * _sigmoid(c), w_ref[0]) + b_ref[0]


def _ada_call(c, w_ada, b_ada):
    depth, d, width = w_ada.shape
    bsz = c.shape[0]
    ncol = width // d
    out = pl.pallas_call(
        _ada_kernel,
        grid=(depth, ncol),
        in_specs=[
            pl.BlockSpec((bsz, d), lambda l, j: (0, 0)),
            pl.BlockSpec((1, d, d), lambda l, j: (l, 0, j)),
            pl.BlockSpec((1, 1, d), lambda l, j: (l, 0, j)),
        ],
        out_specs=pl.BlockSpec((1, bsz, d), lambda l, j: (l, 0, j)),
        out_shape=jax.ShapeDtypeStruct((depth, bsz, width), F32),
        compiler_params=_params("parallel", "parallel"),
        name="ada",
    )(c, w_ada, b_ada.reshape(depth, 1, width))
    return out.reshape(depth, bsz, N_ADA, d)


def _rope_kernel(pos_ref, inv_ref, cos_ref, sin_ref):
    ang = pos_ref[...].astype(F32) * inv_ref[...]
    cos_ref[...] = jnp.cos(ang)
    sin_ref[...] = jnp.sin(ang)


def _rope_call(positions, tm):
    n = positions.size
    half = RET_QK_DIM // 2
    inv_freq = ROPE_BASE ** (-jnp.arange(half, dtype=F32) / half)
    return pl.pallas_call(
        _rope_kernel,
        grid=(n // tm,),
        in_specs=[
            pl.BlockSpec((tm, 1), lambda i: (i, 0)),
            pl.BlockSpec((1, half), lambda i: (0, 0)),
        ],
        out_specs=[pl.BlockSpec((tm, half), lambda i: (i, 0))] * 2,
        out_shape=[jax.ShapeDtypeStruct((n, half), F32)] * 2,
        compiler_params=_params("parallel"),
        name="rope",
    )(positions.reshape(n, 1), inv_freq.reshape(1, half))


def _modulate_kernel(x_ref, ada_ref, h_ref):
    shift = ada_ref[0, 0:1, :]
    scale = ada_ref[0, 1:2, :]
    h_ref[...] = (x_ref[...] * (1.0 + scale) + shift).astype(BF16)


def _modulate_call(xf, ada_l, seq, tm):
    n, d = xf.shape
    per = seq // tm
    return pl.pallas_call(
        _modulate_kernel,
        grid=(n // tm,),
        in_specs=[
            pl.BlockSpec((tm, d), lambda i: (i, 0)),
            pl.BlockSpec((1, N_ADA, d), lambda i: (i // per, 0, 0)),
        ],
        out_specs=pl.BlockSpec((tm, d), lambda i: (i, 0)),
        out_shape=jax.ShapeDtypeStruct((n, d), BF16),
        compiler_params=_params("parallel"),
        name="modulate",
    )(xf, ada_l)


def _glu_kernel(h_ref, wa_ref, wb_ref, u_ref):
    h = h_ref[...]
    a = _bdot(h, wa_ref[...])
    b = _bdot(h, wb_ref[...])
    u_ref[...] = a * _sigmoid(b)


def _glu_call(h, w_in_b, layer, tm):
    n, d = h.shape
    return pl.pallas_call(
        _glu_kernel,
        grid=(n // tm,),
        in_specs=[
            pl.BlockSpec((tm, d), lambda i: (i, 0)),
            pl.BlockSpec((None, d, d), lambda i: (layer, 0, 0)),
            pl.BlockSpec((None, d, d), lambda i: (layer, 0, 1)),
        ],
        out_specs=pl.BlockSpec((tm, d), lambda i: (i, 0)),
        out_shape=jax.ShapeDtypeStruct((n, d), F32),
        compiler_params=_params("parallel"),
        name="proj_glu",
    )(h, w_in_b, w_in_b)


def _qk_kernel(h_ref, w_ref, cos_ref, sin_ref, o_ref):
    acc = _bdot(h_ref[...], w_ref[...])
    scale = jnp.where(pl.program_id(0) == 0, 1.0, RET_QK_DIM ** -0.5).astype(F32)
    cos = cos_ref[...] * scale
    sin = sin_ref[...] * scale
    half = RET_QK_DIM // 2
    for hd in range(RET_HEADS):
        lo = hd * RET_QK_DIM
        x1 = acc[:, lo:lo + half]
        x2 = acc[:, lo + half:lo + RET_QK_DIM]
        o_ref[0, :, lo:lo + half] = (x1 * cos - x2 * sin).astype(BF16)
        o_ref[0, :, lo + half:lo + RET_QK_DIM] = (x2 * cos + x1 * sin).astype(BF16)


def _qk_call(h, w_in_b, layer, cos, sin, tm):
    n, d = h.shape
    half = RET_QK_DIM // 2
    return pl.pallas_call(
        _qk_kernel,
        grid=(2, n // tm),
        in_specs=[
            pl.BlockSpec((tm, d), lambda j, i: (i, 0)),
            pl.BlockSpec((None, d, d), lambda j, i: (layer, 0, 2 + j)),
            pl.BlockSpec((tm, half), lambda j, i: (i, 0)),
            pl.BlockSpec((tm, half), lambda j, i: (i, 0)),
        ],
        out_specs=pl.BlockSpec((1, tm, d), lambda j, i: (j, i, 0)),
        out_shape=jax.ShapeDtypeStruct((2, n, d), BF16),
        compiler_params=_params("parallel", "parallel"),
        name="proj_qk",
    )(h, w_in_b, cos, sin)


def _vgg_kernel(h_ref, w_ref, o_ref):
    j = pl.program_id(0)

    @pl.when(j < 2)
    def _():
        o_ref[...] = _bdot(h_ref[...], w_ref[...]).astype(BF16)

    @pl.when(jnp.logical_and(j >= 2, j < 4))
    def _():
        acc = _bdot(h_ref[...], w_ref[...])
        o_ref[...] = (acc * _sigmoid(acc)).astype(BF16)

    @pl.when(j >= 4)
    def _():
        o_ref[...] = _sigmoid(_bdot(h_ref[...], w_ref[...])).astype(BF16)


def _vgg_call(h, w_in_b, layer, tm):
    n, d = h.shape
    col0, ncols = 4, 6
    return pl.pallas_call(
        _vgg_kernel,
        grid=(ncols, n // tm),
        in_specs=[
            pl.BlockSpec((tm, d), lambda j, i: (i, 0)),
            pl.BlockSpec((None, d, d), lambda j, i: (layer, 0, col0 + j)),
        ],
        out_specs=pl.BlockSpec((tm, d), lambda j, i: (i, j)),
        out_shape=jax.ShapeDtypeStruct((n, ncols * d), BF16),
        compiler_params=_params("parallel", "parallel"),
        name="proj_v_gates",
    )(h, w_in_b)


def _conv_kernel(cur_ref, prev_ref, w_ref, bias_ref, g_ref, b_ref, o_ref, buf_ref, y_ref, *, tt):
    ch = cur_ref.shape[-1]
    first = pl.program_id(1) == 0
    buf_ref[0, 0:CONV_HALO, :] = jnp.where(first, 0.0, prev_ref[0])
    buf_ref[0, CONV_HALO:, :] = cur_ref[0]

    sub = lax.broadcasted_iota(jnp.int32, (SUBLANES, ch), 0)
    n_shift_tiles = (tt + CONV_HALO) // SUBLANES - 1
    for s in range(1, SUBLANES):
        own = sub < SUBLANES - s
        cur = pltpu.roll(buf_ref[0, 0:SUBLANES, :], SUBLANES - s, 0)
        for j in range(n_shift_tiles):
            nxt = pltpu.roll(buf_ref[0, (j + 1) * SUBLANES:(j + 2) * SUBLANES, :], SUBLANES - s, 0)
            buf_ref[s, j * SUBLANES:(j + 1) * SUBLANES, :] = jnp.where(own, cur, nxt)
            cur = nxt

    base = CONV_HALO - (CONV_KERNEL - 1)
    for lg in range(ch // LANES):
        ls = slice(lg * LANES, (lg + 1) * LANES)
        bias = bias_ref[:, ls]
        for r0 in range(0, tt, CONV_ROWS):
            acc = jnp.zeros((CONV_ROWS, LANES), F32)
            for k in range(CONV_KERNEL):
                a, s = divmod(base + k, SUBLANES)
                lo = r0 + a * SUBLANES
                acc = acc + buf_ref[s, lo:lo + CONV_ROWS, ls] * w_ref[k:k + 1, ls]
            y_ref[r0:r0 + CONV_ROWS, ls] = acc + bias

    g = g_ref[...]
    b = b_ref[...]
    for r0 in range(0, tt, NORM_ROWS):
        y = _layer_norm_rows(y_ref[r0:r0 + NORM_ROWS, :], g, b)
        o_ref[0, r0:r0 + NORM_ROWS, :] = (y * _sigmoid(y)).astype(BF16)


def _conv_call(u, w_dw, b_dw, ln_g, ln_b, tt):
    bsz, seq, ch = u.shape
    per = tt // CONV_HALO
    return pl.pallas_call(
        functools.partial(_conv_kernel, tt=tt),
        grid=(bsz, seq // tt),
        in_specs=[
            pl.BlockSpec((1, tt, ch), lambda b, t: (b, t, 0)),
            pl.BlockSpec((1, CONV_HALO, ch), lambda b, t: (b, jnp.maximum(t * per - 1, 0), 0)),
            pl.BlockSpec((CONV_KERNEL, ch), lambda b, t: (0, 0)),
            pl.BlockSpec((1, ch), lambda b, t: (0, 0)),
            pl.BlockSpec((1, ch), lambda b, t: (0, 0)),
            pl.BlockSpec((1, ch), lambda b, t: (0, 0)),
        ],
        out_specs=pl.BlockSpec((1, tt, ch), lambda b, t: (b, t, 0)),
        out_shape=jax.ShapeDtypeStruct((bsz, seq, ch), BF16),
        scratch_shapes=[pltpu.VMEM((SUBLANES, CONV_HALO + tt, ch), F32), pltpu.VMEM((tt, ch), F32)],
        compiler_params=_params("parallel", "parallel"),
        name="conv_ln",
    )(u, u, w_dw, b_dw.reshape(1, ch), ln_g.reshape(1, ch), ln_b.reshape(1, ch))


def _ret_kernel(q_ref, k_ref, v_ref, g_ref, o_ref, s_ref, *, chunk):
    @pl.when(pl.program_id(1) == 0)
    def _():
        s_ref[...] = jnp.zeros_like(s_ref)

    row = lax.broadcasted_iota(jnp.int32, (chunk, 1), 0).astype(F32)
    diff = (lax.broadcasted_iota(jnp.int32, (chunk, chunk), 0)
            - lax.broadcasted_iota(jnp.int32, (chunk, chunk), 1)).astype(F32)
    causal = diff >= 0
    lag = jnp.maximum(diff, 0.0)
    for hd in range(RET_HEADS):
        log_g = math.log1p(-(2.0 ** (-5 - hd)))
        q_decay = jnp.exp(log_g * (row + 1.0))
        k_decay = jnp.exp(log_g * (chunk - 1.0 - row))
        chunk_decay = math.exp(log_g * chunk)
        intra = jnp.where(causal, jnp.exp(log_g * lag), 0.0)

        qs = slice(hd * RET_QK_DIM, (hd + 1) * RET_QK_DIM)
        vs = slice(hd * RET_V_DIM, (hd + 1) * RET_V_DIM)
        q = q_ref[:, qs]
        k = k_ref[:, qs]
        v = v_ref[:, vs]
        state = s_ref[hd]
        scores = lax.dot_general(q, k, (((1,), (1,)), ((), ())), preferred_element_type=F32)
        inner = _bdot((scores * intra).astype(BF16), v)
        cross = _bdot(q, state.astype(BF16)) * q_decay
        update = lax.dot_general((k.astype(F32) * k_decay).astype(BF16), v, (((0,), (0,)), ((), ())),
                                 preferred_element_type=F32)
        s_ref[hd] = state * chunk_decay + update

        o = inner + cross
        mu = jnp.mean(o, axis=-1, keepdims=True)
        d = o - mu
        var = jnp.mean(d * d, axis=-1, keepdims=True)
        o_ref[:, vs] = (g_ref[:, vs] * (d * lax.rsqrt(var + LN_EPS))).astype(BF16)


def _ret_call(qk, vgg, bsz, seq, chunk):
    width = RET_HEADS * RET_QK_DIM
    qk4 = qk.reshape(2, bsz, seq, width)
    v3 = g3 = vgg.reshape(bsz, seq, vgg.shape[-1])
    out = pl.pallas_call(
        functools.partial(_ret_kernel, chunk=chunk),
        grid=(bsz, seq // chunk),
        in_specs=[
            pl.BlockSpec((None, None, chunk, width), lambda b, n: (0, b, n, 0)),
            pl.BlockSpec((None, None, chunk, width), lambda b, n: (1, b, n, 0)),
            pl.BlockSpec((None, chunk, RET_V_WIDTH), lambda b, n: (b, n, 0)),
            pl.BlockSpec((None, chunk, RET_V_WIDTH), lambda b, n: (b, n, 1)),
        ],
        out_specs=pl.BlockSpec((None, chunk, RET_V_WIDTH), lambda b, n: (b, n, 0)),
        out_shape=jax.ShapeDtypeStruct((bsz, seq, RET_V_WIDTH), BF16),
        scratch_shapes=[pltpu.VMEM((RET_HEADS, RET_QK_DIM, RET_V_DIM), F32)],
        compiler_params=_params("parallel", "arbitrary"),
        name="retention",
    )(qk4, qk4, v3, g3)
    return out.reshape(bsz * seq, RET_V_WIDTH)


def _route(logits, carry_ref):
    rows = logits.shape[0]
    lane = lax.broadcasted_iota(jnp.int32, logits.shape, 1)
    neg = jnp.float32(-jnp.inf)
    lg = jnp.where(lane < N_EXPERTS, logits, neg)
    m1 = jnp.max(lg, axis=-1, keepdims=True)
    i1 = jnp.min(jnp.where(lg == m1, lane, LANES), axis=-1, keepdims=True)
    lg2 = jnp.where(lane == i1, neg, lg)
    m2 = jnp.max(lg2, axis=-1, keepdims=True)
    i2 = jnp.min(jnp.where(lg2 == m2, lane, LANES), axis=-1, keepdims=True)
    e = jnp.exp(m2 - m1)
    w1 = 1.0 / (1.0 + e)
    w2 = e / (1.0 + e)

    sel = jnp.where(jnp.logical_or(lane == i1, lane == i2), 1.0, 0.0)
    tri = jnp.where(lax.broadcasted_iota(jnp.int32, (rows, rows), 0)
                    >= lax.broadcasted_iota(jnp.int32, (rows, rows), 1), 1.0, 0.0).astype(BF16)
    incl = _bdot(tri, sel.astype(BF16))
    rank = carry_ref[...] + incl - sel
    carry_ref[...] = carry_ref[...] + incl[rows - 1:rows, :]
    r1 = jnp.sum(jnp.where(lane == i1, rank, 0.0), axis=-1, keepdims=True)
    r2 = jnp.sum(jnp.where(lane == i2, rank, 0.0), axis=-1, keepdims=True)

    rec = jnp.zeros(logits.shape, F32)
    for idx, val in ((R_E1, i1.astype(F32)), (R_E2, i2.astype(F32)), (R_RANK1, r1), (R_RANK2, r2),
                     (R_W1, w1), (R_W2, w2)):
        rec = jnp.where(lane == idx, val, rec)
    return rec


def _mix_kernel(*refs, router):
    if router:
        (uc_ref, og_ref, gt_ref, x_ref, ada_ref, wc_ref, wr_ref, wo_ref, g_ref, b_ref, wrt_ref,
         xo_ref, ho_ref, rec_ref, cnt_ref, carry_ref) = refs
    else:
        (uc_ref, og_ref, gt_ref, x_ref, ada_ref, wc_ref, wr_ref, wo_ref, g_ref, b_ref,
         xo_ref, ho_ref) = refs
    d = x_ref.shape[-1]
    y_conv = _bdot(uc_ref[...], wc_ref[...])
    y_ret = _bdot(og_ref[...], wr_ref[...])
    merged = gt_ref[:, 0:d] * y_conv + gt_ref[:, d:2 * d] * y_ret
    mix = _bdot(merged.astype(BF16), wo_ref[...])
    gate1 = ada_ref[0, 2:3, :]
    shift2 = ada_ref[0, 3:4, :]
    scale2 = ada_ref[0, 4:5, :]
    xn = _layer_norm_rows(DEEPNORM_ALPHA * x_ref[...] + gate1 * mix, g_ref[...], b_ref[...])
    xo_ref[...] = xn
    h2 = xn * (1.0 + scale2) + shift2
    ho_ref[...] = h2.astype(ho_ref.dtype)
    if router:
        @pl.when(pl.program_id(0) == 0)
        def _():
            carry_ref[...] = jnp.zeros_like(carry_ref)

        rec_ref[...] = _route(_dot3(h2, wrt_ref[...]), carry_ref)
        cnt_ref[...] = carry_ref[...]


def _mix_call(uc, og, gates, xf, ada_l, wc, wr, wo, layer, ln_g, ln_b, w_router, seq, tm):
    n, d = xf.shape
    per = seq // tm
    router = w_router is not None
    row = lambda i: (i, 0)
    const = lambda i: (0, 0)
    in_specs = [
        pl.BlockSpec((tm, d), row),
        pl.BlockSpec((tm, RET_V_WIDTH), row),
        pl.BlockSpec((tm, 2 * d), lambda i: (i, 2)),
        pl.BlockSpec((tm, d), row),
        pl.BlockSpec((1, N_ADA, d), lambda i: (i // per, 0, 0)),
        pl.BlockSpec((None, d, d), lambda i: (layer, 0, 0)),
        pl.BlockSpec((None, RET_V_WIDTH, d), lambda i: (layer, 0, 0)),
        pl.BlockSpec((None, d, d), lambda i: (layer, 0, 0)),
        pl.BlockSpec((1, d), const),
        pl.BlockSpec((1, d), const),
    ]
    args = [uc, og, gates, xf, ada_l, wc, wr, wo, ln_g.reshape(1, d), ln_b.reshape(1, d)]
    out_specs = [pl.BlockSpec((tm, d), row), pl.BlockSpec((tm, d), row)]
    out_shape = [jax.ShapeDtypeStruct((n, d), F32), jax.ShapeDtypeStruct((n, d), F32 if router else BF16)]
    scratch = []
    if router:
        wrt = jnp.zeros((d, LANES), F32).at[:, :N_EXPERTS].set(w_router)
        in_specs.append(pl.BlockSpec((d, LANES), const))
        args.append(wrt)
        out_specs += [pl.BlockSpec((tm, LANES), row), pl.BlockSpec((1, LANES), const)]
        out_shape += [jax.ShapeDtypeStruct((n, LANES), F32), jax.ShapeDtypeStruct((1, LANES), F32)]
        scratch = [pltpu.VMEM((1, LANES), F32)]
    return pl.pallas_call(
        functools.partial(_mix_kernel, router=router),
        grid=(n // tm,),
        in_specs=in_specs,
        out_specs=out_specs,
        out_shape=out_shape,
        scratch_shapes=scratch,
        compiler_params=_params("arbitrary" if router else "parallel"),
        name="mix_out_router" if router else "mix_out",
    )(*args)


def _residual_epilogue(ff, x_ref, ada_ref, adan_ref, g_ref, b_ref, xo_ref, ho_ref):
    gate2 = ada_ref[0, 5:6, :]
    xn = _layer_norm_rows(DEEPNORM_ALPHA * x_ref[...] + gate2 * ff, g_ref[...], b_ref[...])
    xo_ref[...] = xn
    if ho_ref is not None:
        shift1 = adan_ref[0, 0:1, :]
        scale1 = adan_ref[0, 1:2, :]
        ho_ref[...] = (xn * (1.0 + scale1) + shift1).astype(BF16)


def _swiglu_cols(h, wg_ref, wu_ref, wd_ref, lo, hi):
    gate = _bdot(h, wg_ref[0, :, lo:hi])
    up = _bdot(h, wu_ref[0, :, lo:hi])
    return _bdot((gate * _sigmoid(gate) * up).astype(BF16), wd_ref[0, lo:hi, :])


def _ffn_kernel(*refs, emit_h):
    if emit_h:
        h_ref, wg_ref, wu_ref, wd_ref, x_ref, ada_ref, adan_ref, g_ref, b_ref, xo_ref, ho_ref, acc_ref = refs
    else:
        h_ref, wg_ref, wu_ref, wd_ref, x_ref, ada_ref, g_ref, b_ref, xo_ref, acc_ref = refs
        adan_ref = ho_ref = None
    c = pl.program_id(1)

    @pl.when(c == 0)
    def _():
        acc_ref[...] = jnp.zeros_like(acc_ref)

    acc_ref[...] += _swiglu_cols(h_ref[...], wg_ref, wu_ref, wd_ref, 0, wg_ref.shape[-1])

    @pl.when(c == pl.num_programs(1) - 1)
    def _():
        _residual_epilogue(acc_ref[...], x_ref, ada_ref, adan_ref, g_ref, b_ref, xo_ref, ho_ref)


def _ffn_call(h, wg, wu, wd, layer, xf, ada_l, ada_next, ln_g, ln_b, seq, tm, fc):
    n, d = xf.shape
    dff = wg.shape[-1]
    per = seq // tm
    emit_h = ada_next is not None
    row = lambda i, c: (i, 0)
    const = lambda i, c: (0, 0)
    ada_map = lambda i, c: (i // per, 0, 0)
    in_specs = [
        pl.BlockSpec((tm, d), row),
        pl.BlockSpec((1, d, fc), lambda i, c: (layer, 0, c)),
        pl.BlockSpec((1, d, fc), lambda i, c: (layer, 0, c)),
        pl.BlockSpec((1, fc, d), lambda i, c: (layer, c, 0)),
        pl.BlockSpec((tm, d), row),
        pl.BlockSpec((1, N_ADA, d), ada_map),
    ]
    args = [h, wg, wu, wd, xf, ada_l]
    if emit_h:
        in_specs.append(pl.BlockSpec((1, N_ADA, d), ada_map))
        args.append(ada_next)
    in_specs += [pl.BlockSpec((1, d), const), pl.BlockSpec((1, d), const)]
    args += [ln_g.reshape(1, d), ln_b.reshape(1, d)]
    out_specs = [pl.BlockSpec((tm, d), row)]
    out_shape = [jax.ShapeDtypeStruct((n, d), F32)]
    if emit_h:
        out_specs.append(pl.BlockSpec((tm, d), row))
        out_shape.append(jax.ShapeDtypeStruct((n, d), BF16))
    outs = pl.pallas_call(
        functools.partial(_ffn_kernel, emit_h=emit_h),
        grid=(n // tm, dff // fc),
        in_specs=in_specs,
        out_specs=out_specs,
        out_shape=out_shape,
        scratch_shapes=[pltpu.VMEM((tm, d), F32)],
        compiler_params=_params("parallel", "arbitrary"),
        name="dense_ffn",
    )(*args)
    return outs if emit_h else (outs[0], None)


def _row_copy(src, src_row, dst, dst_row, sem):
    return pltpu.make_async_copy(src.at[pl.ds(src_row, 1)], dst.at[pl.ds(dst_row, 1)], sem)


def _scatter_kernel(zt_ref, p1_ref, p2_ref, hf_ref, xs_hbm, zero_ref, sem):
    i = pl.program_id(0)
    tb = p1_ref.shape[0]
    tile = zero_ref.shape[0]

    def zero_copy(j):
        return pltpu.make_async_copy(zero_ref, xs_hbm.at[pl.ds(zt_ref[j] * tile, tile)], sem)

    @pl.when(i == 0)
    def _():
        zero_ref[...] = jnp.zeros_like(zero_ref)
        for j in range(zt_ref.shape[0]):
            pl.when(zt_ref[j] >= 0)(lambda j=j: zero_copy(j).start())
        for j in range(zt_ref.shape[0]):
            pl.when(zt_ref[j] >= 0)(lambda j=j: zero_copy(j).wait())

    def start(k, carry):
        _row_copy(hf_ref, k, xs_hbm, p1_ref[k], sem).start()
        _row_copy(hf_ref, k, xs_hbm, p2_ref[k], sem).start()
        return carry

    def wait(k, carry):
        _row_copy(hf_ref, k, xs_hbm, p1_ref[k], sem).wait()
        _row_copy(hf_ref, k, xs_hbm, p2_ref[k], sem).wait()
        return carry

    lax.fori_loop(0, tb, start, 0, unroll=8)
    lax.fori_loop(0, tb, wait, 0, unroll=8)


def _scatter_call(hf, p1, p2, zero_tiles, rows, tile, tb):
    n, d = hf.shape
    return pl.pallas_call(
        _scatter_kernel,
        grid_spec=pltpu.PrefetchScalarGridSpec(
            num_scalar_prefetch=1,
            grid=(n // tb,),
            in_specs=[
                pl.BlockSpec((tb,), lambda i, zt: (i,), memory_space=pltpu.SMEM),
                pl.BlockSpec((tb,), lambda i, zt: (i,), memory_space=pltpu.SMEM),
                pl.BlockSpec((tb, d), lambda i, zt: (i, 0)),
            ],
            out_specs=pl.BlockSpec(memory_space=pl.ANY),
            scratch_shapes=[pltpu.VMEM((tile, d), F32), pltpu.SemaphoreType.DMA(())],
        ),
        out_shape=jax.ShapeDtypeStruct((rows, d), F32),
        compiler_params=_params("arbitrary"),
        name="moe_scatter",
    )(zero_tiles, p1, p2, hf)


def _expert_kernel(te_ref, xs_ref, wg_ref, wu_ref, wd_ref, ys_ref):
    h = xs_ref[...].astype(BF16)
    half = wg_ref.shape[-1] // 2
    ys_ref[...] = (_swiglu_cols(h, wg_ref, wu_ref, wd_ref, 0, half)
                   + _swiglu_cols(h, wg_ref, wu_ref, wd_ref, half, 2 * half))


def _expert_call(xs, wg, wu, wd, tile_expert, tile):
    rows, d = xs.shape
    dff = wg.shape[-1]
    return pl.pallas_call(
        _expert_kernel,
        grid_spec=pltpu.PrefetchScalarGridSpec(
            num_scalar_prefetch=1,
            grid=(rows // tile,),
            in_specs=[
                pl.BlockSpec((tile, d), lambda r, te: (r, 0)),
                pl.BlockSpec((1, d, dff), lambda r, te: (te[r], 0, 0)),
                pl.BlockSpec((1, d, dff), lambda r, te: (te[r], 0, 0)),
                pl.BlockSpec((1, dff, d), lambda r, te: (te[r], 0, 0)),
            ],
            out_specs=pl.BlockSpec((tile, d), lambda r, te: (r, 0)),
        ),
        out_shape=jax.ShapeDtypeStruct((rows, d), F32),
        compiler_params=_params("arbitrary"),
        name="moe_experts",
    )(tile_expert, xs, wg, wu, wd)


def _combine_kernel(*refs, emit_h):
    if emit_h:
        (p1_ref, p2_ref, p1n_ref, p2n_ref, ys_hbm, rec_ref, x_ref, ada_ref, adan_ref, g_ref, b_ref,
         xo_ref, ho_ref, buf_ref, sem) = refs
    else:
        (p1_ref, p2_ref, p1n_ref, p2n_ref, ys_hbm, rec_ref, x_ref, ada_ref, g_ref, b_ref,
         xo_ref, buf_ref, sem) = refs
        adan_ref = ho_ref = None
    tm = x_ref.shape[0]
    i = pl.program_id(0)
    slot = i % 2

    def rows(pa_ref, pb_ref, s, k):
        return (_row_copy(ys_hbm, pa_ref[k], buf_ref.at[s, 0], k, sem.at[s]),
                _row_copy(ys_hbm, pb_ref[k], buf_ref.at[s, 1], k, sem.at[s]))

    def start_tile(pa_ref, pb_ref, s):
        def body(k, carry):
            a, b = rows(pa_ref, pb_ref, s, k)
            a.start()
            b.start()
            return carry
        lax.fori_loop(0, tm, body, 0, unroll=8)

    @pl.when(i == 0)
    def _():
        start_tile(p1_ref, p2_ref, 0)

    @pl.when(i + 1 < pl.num_programs(0))
    def _():
        start_tile(p1n_ref, p2n_ref, 1 - slot)

    def wait_body(k, carry):
        a, b = rows(p1_ref, p2_ref, slot, k)
        a.wait()
        b.wait()
        return carry

    lax.fori_loop(0, tm, wait_body, 0, unroll=8)
    w1 = rec_ref[:, R_W1:R_W1 + 1]
    w2 = rec_ref[:, R_W2:R_W2 + 1]
    ff = w1 * buf_ref[slot, 0] + w2 * buf_ref[slot, 1]
    _residual_epilogue(ff, x_ref, ada_ref, adan_ref, g_ref, b_ref, xo_ref, ho_ref)


def _combine_call(ys, p1, p2, rec, xf, ada_l, ada_next, ln_g, ln_b, seq, tm):
    n, d = xf.shape
    per = seq // tm
    emit_h = ada_next is not None
    row = lambda i: (i, 0)
    const = lambda i: (0, 0)
    ada_map = lambda i: (i // per, 0, 0)
    last = n // tm - 1
    cur = lambda i: (i,)
    nxt = lambda i: (jnp.minimum(i + 1, last),)
    in_specs = [
        pl.BlockSpec((tm,), cur, memory_space=pltpu.SMEM),
        pl.BlockSpec((tm,), cur, memory_space=pltpu.SMEM),
        pl.BlockSpec((tm,), nxt, memory_space=pltpu.SMEM),
        pl.BlockSpec((tm,), nxt, memory_space=pltpu.SMEM),
        pl.BlockSpec(memory_space=pl.ANY),
        pl.BlockSpec((tm, LANES), row),
        pl.BlockSpec((tm, d), row),
        pl.BlockSpec((1, N_ADA, d), ada_map),
    ]
    args = [p1, p2, p1, p2, ys, rec, xf, ada_l]
    if emit_h:
        in_specs.append(pl.BlockSpec((1, N_ADA, d), ada_map))
        args.append(ada_next)
    in_specs += [pl.BlockSpec((1, d), const), pl.BlockSpec((1, d), const)]
    args += [ln_g.reshape(1, d), ln_b.reshape(1, d)]
    out_specs = [pl.BlockSpec((tm, d), row)]
    out_shape = [jax.ShapeDtypeStruct((n, d), F32)]
    if emit_h:
        out_specs.append(pl.BlockSpec((tm, d), row))
        out_shape.append(jax.ShapeDtypeStruct((n, d), BF16))
    outs = pl.pallas_call(
        functools.partial(_combine_kernel, emit_h=emit_h),
        grid=(n // tm,),
        in_specs=in_specs,
        out_specs=out_specs,
        out_shape=out_shape,
        scratch_shapes=[pltpu.VMEM((2, 2, tm, d), F32), pltpu.SemaphoreType.DMA((2,))],
        compiler_params=_params("arbitrary"),
        name="moe_combine",
    )(*args)
    return outs if emit_h else (outs[0], None)


def _moe_call(hf, rec, counts, wg, wu, wd, expert_base, xf, ada_l, ada_next, ln_g, ln_b, seq, tile, tm):
    n = hf.shape[0]
    total_tiles = 2 * n // tile + N_EXPERTS
    cnt = counts[0, :N_EXPERTS].astype(jnp.int32)
    ntiles = (cnt + tile - 1) // tile
    end_tile = jnp.cumsum(ntiles)
    start_row = (end_tile - ntiles) * tile
    p1 = start_row[rec[:, R_E1].astype(jnp.int32)] + rec[:, R_RANK1].astype(jnp.int32)
    p2 = start_row[rec[:, R_E2].astype(jnp.int32)] + rec[:, R_RANK2].astype(jnp.int32)
    tile_ids = jnp.arange(total_tiles, dtype=jnp.int32)
    tile_expert = jnp.minimum(jnp.sum(tile_ids[:, None] >= end_tile[None, :], axis=1), N_EXPERTS - 1)
    partial_last = jnp.where(cnt % tile != 0, end_tile - 1, -1)
    tail = end_tile[-1] + jnp.arange(N_EXPERTS, dtype=jnp.int32)
    zero_tiles = jnp.concatenate([partial_last, jnp.where(tail < total_tiles, tail, -1)]).astype(jnp.int32)

    xs = _scatter_call(hf, p1, p2, zero_tiles, total_tiles * tile, tile, tm)
    ys = _expert_call(xs, wg, wu, wd, (tile_expert + expert_base).astype(jnp.int32), tile)
    return _combine_call(ys, p1, p2, rec, xf, ada_l, ada_next, ln_g, ln_b, seq, tm)


def kernel(x, c, positions, w_ada, b_ada, w_in, w_dw, b_dw, ln_conv_g, ln_conv_b, w_conv_o, w_ret_o, w_out, ln1_g, ln1_b, ffn_w_gate, ffn_w_up, ffn_w_down, moe_w_router, moe_w_gate, moe_w_up, moe_w_down, ln2_g, ln2_b):
    bsz, seq, d = x.shape
    n = bsz * seq
    tm = 512
    tm_proj = 1024
    tm_mix = 256
    fc = D_FF // 2
    tt = 256
    ret_chunk = 256
    moe_tile = 256

    w_in_b = _cast_call(w_in)
    w_conv_o_b = _cast_call(w_conv_o)
    w_ret_o_b = _cast_call(w_ret_o)
    w_out_b = _cast_call(w_out)
    ffn_b = [_cast_call(w) for w in (ffn_w_gate, ffn_w_up, ffn_w_down)]
    moe_b = [_cast_call(w.reshape((-1,) + w.shape[2:])) for w in (moe_w_gate, moe_w_up, moe_w_down)]

    xf = x.reshape(n, d)
    ada = _ada_call(c, w_ada, b_ada)
    cos, sin = _rope_call(positions, tm)
    h = _modulate_call(xf, ada[0], seq, tm)

    for l in range(DEPTH):
        u = _glu_call(h, w_in_b, l, tm_proj)
        qk = _qk_call(h, w_in_b, l, cos, sin, tm_proj)
        vgg = _vgg_call(h, w_in_b, l, tm_proj)

        uc = _conv_call(u.reshape(bsz, seq, d), w_dw[l], b_dw[l], ln_conv_g[l], ln_conv_b[l], tt)
        og = _ret_call(qk, vgg, bsz, seq, ret_chunk)

        i = l // 2
        is_moe = l % 2 == 1
        mixed = _mix_call(
            uc.reshape(n, d), og, vgg, xf, ada[l], w_conv_o_b, w_ret_o_b, w_out_b, l,
            ln1_g[l], ln1_b[l], moe_w_router[i] if is_moe else None, seq, tm_mix)
        ada_next = ada[l + 1] if l + 1 < DEPTH else None
        if is_moe:
            xf, hf, rec, counts = mixed
            xf, h = _moe_call(hf, rec, counts, *moe_b, i * N_EXPERTS, xf, ada[l], ada_next,
                              ln2_g[l], ln2_b[l], seq, moe_tile, tm)
        else:
            xf, h2 = mixed
            xf, h = _ffn_call(h2, *ffn_b, i, xf, ada[l], ada_next, ln2_g[l], ln2_b[l], seq, tm, fc)
    return xf.reshape(bsz, seq, d)
```

```python
import functools
import math

import jax
import jax.numpy as jnp
from jax import lax
from jax.experimental import pallas as pl
from jax.experimental.pallas import tpu as pltpu

D_MODEL = 1024
DEPTH = 4
CONV_KERNEL = 31
RET_QK_DIM = 256
RET_V_DIM = 512
RET_HEADS = 4
RET_V_WIDTH = RET_HEADS * RET_V_DIM
RET_CHUNK = 128
ROPE_BASE = 10000.0
D_FF = 2816
N_EXPERTS = 8
N_ADA = 6
DEEPNORM_ALPHA = (2 * DEPTH) ** 0.25
LN_EPS = 1e-5

LANES = 128
SUBLANES = 8
CONV_HALO = 32
CONV_ROWS = 128
NORM_ROWS = 32
CAST_ROWS = 512
CAST_MAX_COLS = D_FF
VMEM_LIMIT = 56 * 1024 * 1024

R_E1, R_E2, R_RANK1, R_RANK2, R_W1, R_W2 = range(6)

F32 = jnp.float32
BF16 = jnp.bfloat16


def _params(*sem):
    return pltpu.CompilerParams(dimension_semantics=sem, vmem_limit_bytes=VMEM_LIMIT)


def _bdot(a, b):
    return jnp.dot(a, b, preferred_element_type=F32)


def _split_bf16(a):
    hi = a.astype(BF16)
    lo = (a - hi.astype(F32)).astype(BF16)
    return hi, lo


def _dot3(a, b):
    a_hi, a_lo = _split_bf16(a)
    b_hi, b_lo = _split_bf16(b)
    return _bdot(a_hi, b_hi) + (_bdot(a_hi, b_lo) + _bdot(a_lo, b_hi))


def _sigmoid(x):
    return jax.nn.sigmoid(x)


def _layer_norm_rows(y, g, b):
    mu = jnp.mean(y, axis=-1, keepdims=True)
    d = y - mu
    var = jnp.mean(d * d, axis=-1, keepdims=True)
    return d * lax.rsqrt(var + LN_EPS) * g + b


def _cast_kernel(w_ref, o_ref):
    o_ref[...] = w_ref[...].astype(BF16)


def _cast_call(w):
    cols = w.shape[-1]
    w2 = w.reshape(-1, cols)
    rows = w2.shape[0]
    bc = cols
    while bc > CAST_MAX_COLS:
        bc //= 2
    assert rows % CAST_ROWS == 0 and cols % bc == 0 and bc % LANES == 0, (w.shape, bc)
    out = pl.pallas_call(
        _cast_kernel,
        grid=(rows // CAST_ROWS, cols // bc),
        in_specs=[pl.BlockSpec((CAST_ROWS, bc), lambda i, j: (i, j))],
        out_specs=pl.BlockSpec((CAST_ROWS, bc), lambda i, j: (i, j)),
        out_shape=jax.ShapeDtypeStruct((rows, cols), BF16),
        compiler_params=_params("parallel", "parallel"),
        name="cast_bf16",
    )(w2)
    return out.reshape(w.shape)


def _ada_kernel(c_ref, w_ref, b_ref, o_ref):
    c = c_ref[...]
    o_ref[0] = _dot3(c * _sigmoid(c), w_ref[0]) + b_ref[0]


def _ada_call(c, w_ada, b_ada):
    depth, d, width = w_ada.shape
    bsz = c.shape[0]
    ncol = width // d
    out = pl.pallas_call(
        _ada_kernel,
        grid=(depth, ncol),
        in_specs=[
            pl.BlockSpec((bsz, d), lambda l, j: (0, 0)),
            pl.BlockSpec((1, d, d), lambda l, j: (l, 0, j)),
            pl.BlockSpec((1, 1, d), lambda l, j: (l, 0, j)),
        ],
        out_specs=pl.BlockSpec((1, bsz, d), lambda l, j: (l, 0, j)),
        out_shape=jax.ShapeDtypeStruct((depth, bsz, width), F32),
        compiler_params=_params("parallel", "parallel"),
        name="ada",
    )(c, w_ada, b_ada.reshape(depth, 1, width))
    return out.reshape(depth, bsz, N_ADA, d)


def _rope_kernel(pos_ref, inv_ref, cos_ref, sin_ref):
    ang = pos_ref[...].astype(F32) * inv_ref[...]
    cos_ref[...] = jnp.cos(ang)
    sin_ref[...] = jnp.sin(ang)


def _rope_call(positions, tm):
    n = positions.size
    half = RET_QK_DIM // 2
    inv_freq = ROPE_BASE ** (-jnp.arange(half, dtype=F32) / half)
    return pl.pallas_call(
        _rope_kernel,
        grid=(n // tm,),
        in_specs=[
            pl.BlockSpec((tm, 1), lambda i: (i, 0)),
            pl.BlockSpec((1, half), lambda i: (0, 0)),
        ],
        out_specs=[pl.BlockSpec((tm, half), lambda i: (i, 0))] * 2,
        out_shape=[jax.ShapeDtypeStruct((n, half), F32)] * 2,
        compiler_params=_params("parallel"),
        name="rope",
    )(positions.reshape(n, 1), inv_freq.reshape(1, half))


def _modulate_kernel(x_ref, ada_ref, h_ref):
    shift = ada_ref[0, 0:1, :]
    scale = ada_ref[0, 1:2, :]
    h_ref[...] = (x_ref[...] * (1.0 + scale) + shift).astype(BF16)


def _modulate_call(xf, ada_l, seq, tm):
    n, d = xf.shape
    per = seq // tm
    return pl.pallas_call(
        _modulate_kernel,
        grid=(n // tm,),
        in_specs=[
            pl.BlockSpec((tm, d), lambda i: (i, 0)),
            pl.BlockSpec((1, N_ADA, d), lambda i: (i // per, 0, 0)),
        ],
        out_specs=pl.BlockSpec((tm, d), lambda i: (i, 0)),
        out_shape=jax.ShapeDtypeStruct((n, d), BF16),
        compiler_params=_params("parallel"),
        name="modulate",
    )(xf, ada_l)


def _qk_kernel(h_ref, w_ref, cos_ref, sin_ref, o_ref):
    acc = _bdot(h_ref[...], w_ref[...])
    scale = jnp.where(pl.program_id(0) == 0, 1.0, RET_QK_DIM ** -0.5).astype(F32)
    cos = cos_ref[...] * scale
    sin = sin_ref[...] * scale
    half = RET_QK_DIM // 2
    for hd in range(RET_HEADS):
        lo = hd * RET_QK_DIM
        x1 = acc[:, lo:lo + half]
        x2 = acc[:, lo + half:lo + RET_QK_DIM]
        o_ref[0, :, lo:lo + half] = (x1 * cos - x2 * sin).astype(BF16)
        o_ref[0, :, lo + half:lo + RET_QK_DIM] = (x2 * cos + x1 * sin).astype(BF16)


def _qk_call(h, w_in_b, layer, cos, sin, tm):
    n, d = h.shape
    half = RET_QK_DIM // 2
    return pl.pallas_call(
        _qk_kernel,
        grid=(2, n // tm),
        in_specs=[
            pl.BlockSpec((tm, d), lambda j, i: (i, 0)),
            pl.BlockSpec((None, d, d), lambda j, i: (layer, 0, 2 + j)),
            pl.BlockSpec((tm, half), lambda j, i: (i, 0)),
            pl.BlockSpec((tm, half), lambda j, i: (i, 0)),
        ],
        out_specs=pl.BlockSpec((1, tm, d), lambda j, i: (j, i, 0)),
        out_shape=jax.ShapeDtypeStruct((2, n, d), BF16),
        compiler_params=_params("parallel", "parallel"),
        name="proj_qk",
    )(h, w_in_b, cos, sin)


def _vgg_kernel(h_ref, w_ref, o_ref):
    j = pl.program_id(0)

    @pl.when(j < 2)
    def _():
        o_ref[...] = _bdot(h_ref[...], w_ref[...]).astype(BF16)

    @pl.when(jnp.logical_and(j >= 2, j < 4))
    def _():
        acc = _bdot(h_ref[...], w_ref[...])
        o_ref[...] = (acc * _sigmoid(acc)).astype(BF16)

    @pl.when(j >= 4)
    def _():
        o_ref[...] = _sigmoid(_bdot(h_ref[...], w_ref[...])).astype(BF16)


def _vgg_call(h, w_in_b, layer, tm):
    n, d = h.shape
    col0, ncols = 4, 6
    return pl.pallas_call(
        _vgg_kernel,
        grid=(ncols, n // tm),
        in_specs=[
            pl.BlockSpec((tm, d), lambda j, i: (i, 0)),
            pl.BlockSpec((None, d, d), lambda j, i: (layer, 0, col0 + j)),
        ],
        out_specs=pl.BlockSpec((tm, d), lambda j, i: (i, j)),
        out_shape=jax.ShapeDtypeStruct((n, ncols * d), BF16),
        compiler_params=_params("parallel", "parallel"),
        name="proj_v_gates",
    )(h, w_in_b)


def _conv_kernel(h_ref, wa_ref, wb_ref, w_ref, bias_ref, g_ref, b_ref, o_ref, u_ref, buf_ref, y_ref, *, tt):
    ch = o_ref.shape[-1]
    t = pl.program_id(1)

    @pl.when(t == 0)
    def _():
        u_ref[...] = jnp.zeros_like(u_ref)
        buf_ref[0, tt:, :] = jnp.zeros((CONV_HALO, ch), F32)

    buf_ref[0, 0:CONV_HALO, :] = jnp.where(t <= 1, 0.0, buf_ref[0, tt:, :])
    buf_ref[0, CONV_HALO:, :] = u_ref[...]

    h = h_ref[...]
    u_ref[...] = _bdot(h, wa_ref[...]) * _sigmoid(_bdot(h, wb_ref[...]))

    sub = lax.broadcasted_iota(jnp.int32, (SUBLANES, ch), 0)
    n_shift_tiles = (tt + CONV_HALO) // SUBLANES - 1
    for s in range(1, SUBLANES):
        own = sub < SUBLANES - s
        cur = pltpu.roll(buf_ref[0, 0:SUBLANES, :], SUBLANES - s, 0)
        for j in range(n_shift_tiles):
            nxt = pltpu.roll(buf_ref[0, (j + 1) * SUBLANES:(j + 2) * SUBLANES, :], SUBLANES - s, 0)
            buf_ref[s, j * SUBLANES:(j + 1) * SUBLANES, :] = jnp.where(own, cur, nxt)
            cur = nxt

    base = CONV_HALO - (CONV_KERNEL - 1)
    for lg in range(ch // LANES):
        ls = slice(lg * LANES, (lg + 1) * LANES)
        bias = bias_ref[:, ls]
        for r0 in range(0, tt, CONV_ROWS):
            acc = jnp.zeros((CONV_ROWS, LANES), F32)
            for k in range(CONV_KERNEL):
                a, s = divmod(base + k, SUBLANES)
                lo = r0 + a * SUBLANES
                acc = acc + buf_ref[s, lo:lo + CONV_ROWS, ls] * w_ref[k:k + 1, ls]
            y_ref[r0:r0 + CONV_ROWS, ls] = acc + bias

    g = g_ref[...]
    b = b_ref[...]
    for r0 in range(0, tt, NORM_ROWS):
        y = _layer_norm_rows(y_ref[r0:r0 + NORM_ROWS, :], g, b)
        o_ref[0, r0:r0 + NORM_ROWS, :] = (y * _sigmoid(y)).astype(BF16)


def _conv_call(h, w_in_b, layer, w_dw, b_dw, ln_g, ln_b, bsz, seq, tt):
    n, d = h.shape
    ch = w_dw.shape[-1]
    nt = seq // tt
    const = lambda b, t: (0, 0)
    return pl.pallas_call(
        functools.partial(_conv_kernel, tt=tt),
        grid=(bsz, nt + 1),
        in_specs=[
            pl.BlockSpec((tt, d), lambda b, t: (b * nt + jnp.minimum(t, nt - 1), 0)),
            pl.BlockSpec((None, d, ch), lambda b, t: (layer, 0, 0)),
            pl.BlockSpec((None, d, ch), lambda b, t: (layer, 0, 1)),
            pl.BlockSpec((CONV_KERNEL, ch), const),
            pl.BlockSpec((1, ch), const),
            pl.BlockSpec((1, ch), const),
            pl.BlockSpec((1, ch), const),
        ],
        out_specs=pl.BlockSpec((1, tt, ch), lambda b, t: (b, jnp.maximum(t - 1, 0), 0)),
        out_shape=jax.ShapeDtypeStruct((bsz, seq, ch), BF16),
        scratch_shapes=[pltpu.VMEM((tt, ch), F32),
                        pltpu.VMEM((SUBLANES, CONV_HALO + tt, ch), F32),
                        pltpu.VMEM((tt, ch), F32)],
        compiler_params=_params("parallel", "arbitrary"),
        name="conv_glu_ln",
    )(h, w_in_b, w_in_b, w_dw, b_dw.reshape(1, ch), ln_g.reshape(1, ch), ln_b.reshape(1, ch))


def _ret_kernel(q_ref, k_ref, v_ref, g_ref, o_ref, s_ref, intra_ref, *, chunk):
    log_gamma = [math.log1p(-(2.0 ** (-5 - hd))) for hd in range(RET_HEADS)]

    @pl.when(pl.program_id(1) == 0)
    def _():
        s_ref[...] = jnp.zeros_like(s_ref)
        diff = (lax.broadcasted_iota(jnp.int32, (chunk, chunk), 0)
                - lax.broadcasted_iota(jnp.int32, (chunk, chunk), 1)).astype(F32)
        for hd in range(RET_HEADS):
            intra_ref[hd] = jnp.where(diff >= 0, jnp.exp(log_gamma[hd] * jnp.maximum(diff, 0.0)), 0.0)

    row = lax.broadcasted_iota(jnp.int32, (chunk, 1), 0).astype(F32)
    for hd in range(RET_HEADS):
        log_g = log_gamma[hd]
        q_decay = jnp.exp(log_g * (row + 1.0))
        k_decay = jnp.exp(log_g * (chunk - 1.0 - row))
        chunk_decay = math.exp(log_g * chunk)
        intra = intra_ref[hd]

        qs = slice(hd * RET_QK_DIM, (hd + 1) * RET_QK_DIM)
        vs = slice(hd * RET_V_DIM, (hd + 1) * RET_V_DIM)
        q = q_ref[:, qs]
        k = k_ref[:, qs]
        v = v_ref[:, vs]
        state = s_ref[hd]
        scores = lax.dot_general(q, k, (((1,), (1,)), ((), ())), preferred_element_type=F32)
        inner = _bdot((scores * intra).astype(BF16), v)
        cross = _bdot(q, state.astype(BF16)) * q_decay
        update = lax.dot_general((k.astype(F32) * k_decay).astype(BF16), v, (((0,), (0,)), ((), ())),
                                 preferred_element_type=F32)
        s_ref[hd] = state * chunk_decay + update

        o = inner + cross
        mu = jnp.mean(o, axis=-1, keepdims=True)
        d = o - mu
        var = jnp.mean(d * d, axis=-1, keepdims=True)
        o_ref[:, vs] = (g_ref[:, vs] * (d * lax.rsqrt(var + LN_EPS))).astype(BF16)


def _ret_call(qk, vgg, bsz, seq, chunk):
    width = RET_HEADS * RET_QK_DIM
    qk4 = qk.reshape(2, bsz, seq, width)
    v3 = g3 = vgg.reshape(bsz, seq, vgg.shape[-1])
    out = pl.pallas_call(
        functools.partial(_ret_kernel, chunk=chunk),
        grid=(bsz, seq // chunk),
        in_specs=[
            pl.BlockSpec((None, None, chunk, width), lambda b, n: (0, b, n, 0)),
            pl.BlockSpec((None, None, chunk, width), lambda b, n: (1, b, n, 0)),
            pl.BlockSpec((None, chunk, RET_V_WIDTH), lambda b, n: (b, n, 0)),
            pl.BlockSpec((None, chunk, RET_V_WIDTH), lambda b, n: (b, n, 1)),
        ],
        out_specs=pl.BlockSpec((None, chunk, RET_V_WIDTH), lambda b, n: (b, n, 0)),
        out_shape=jax.ShapeDtypeStruct((bsz, seq, RET_V_WIDTH), BF16),
        scratch_shapes=[pltpu.VMEM((RET_HEADS, RET_QK_DIM, RET_V_DIM), F32),
                        pltpu.VMEM((RET_HEADS, chunk, chunk), F32)],
        compiler_params=_params("parallel", "arbitrary"),
        name="retention",
    )(qk4, qk4, v3, g3)
    return out.reshape(bsz * seq, RET_V_WIDTH)


def _route(logits, carry_ref):
    rows = logits.shape[0]
    lane = lax.broadcasted_iota(jnp.int32, logits.shape, 1)
    neg = jnp.float32(-jnp.inf)
    lg = jnp.where(lane < N_EXPERTS, logits, neg)
    m1 = jnp.max(lg, axis=-1, keepdims=True)
    i1 = jnp.min(jnp.where(lg == m1, lane, LANES), axis=-1, keepdims=True)
    lg2 = jnp.where(lane == i1, neg, lg)
    m2 = jnp.max(lg2, axis=-1, keepdims=True)
    i2 = jnp.min(jnp.where(lg2 == m2, lane, LANES), axis=-1, keepdims=True)
    e = jnp.exp(m2 - m1)
    w1 = 1.0 / (1.0 + e)
    w2 = e / (1.0 + e)

    sel = jnp.where(jnp.logical_or(lane == i1, lane == i2), 1.0, 0.0)
    tri = jnp.where(lax.broadcasted_iota(jnp.int32, (rows, rows), 0)
                    >= lax.broadcasted_iota(jnp.int32, (rows, rows), 1), 1.0, 0.0).astype(BF16)
    incl = _bdot(tri, sel.astype(BF16))
    rank = carry_ref[...] + incl - sel
    carry_ref[...] = carry_ref[...] + incl[rows - 1:rows, :]
    r1 = jnp.sum(jnp.where(lane == i1, rank, 0.0), axis=-1, keepdims=True)
    r2 = jnp.sum(jnp.where(lane == i2, rank, 0.0), axis=-1, keepdims=True)

    rec = jnp.zeros(logits.shape, F32)
    for idx, val in ((R_E1, i1.astype(F32)), (R_E2, i2.astype(F32)), (R_RANK1, r1), (R_RANK2, r2),
                     (R_W1, w1), (R_W2, w2)):
        rec = jnp.where(lane == idx, val, rec)
    return rec


def _mix_kernel(*refs, router):
    if router:
        (uc_ref, og_ref, gt_ref, x_ref, ada_ref, wc_ref, wr_ref, wo_ref, g_ref, b_ref, wrt_ref,
         xo_ref, ho_ref, rec_ref, cnt_ref, carry_ref) = refs
    else:
        (uc_ref, og_ref, gt_ref, x_ref, ada_ref, wc_ref, wr_ref, wo_ref, g_ref, b_ref,
         xo_ref, ho_ref) = refs
    d = x_ref.shape[-1]
    y_conv = _bdot(uc_ref[...], wc_ref[...])
    y_ret = _bdot(og_ref[...], wr_ref[...])
    merged = gt_ref[:, 0:d] * y_conv + gt_ref[:, d:2 * d] * y_ret
    mix = _bdot(merged.astype(BF16), wo_ref[...])
    gate1 = ada_ref[0, 2:3, :]
    shift2 = ada_ref[0, 3:4, :]
    scale2 = ada_ref[0, 4:5, :]
    xn = _layer_norm_rows(DEEPNORM_ALPHA * x_ref[...] + gate1 * mix, g_ref[...], b_ref[...])
    xo_ref[...] = xn
    h2 = xn * (1.0 + scale2) + shift2
    ho_ref[...] = h2.astype(ho_ref.dtype)
    if router:
        @pl.when(pl.program_id(0) == 0)
        def _():
            carry_ref[...] = jnp.zeros_like(carry_ref)

        rec_ref[...] = _route(_dot3(h2, wrt_ref[...]), carry_ref)
        cnt_ref[...] = carry_ref[...]


def _mix_call(uc, og, gates, xf, ada_l, wc, wr, wo, layer, ln_g, ln_b, w_router, seq, tm):
    n, d = xf.shape
    per = seq // tm
    router = w_router is not None
    row = lambda i: (i, 0)
    const = lambda i: (0, 0)
    in_specs = [
        pl.BlockSpec((tm, d), row),
        pl.BlockSpec((tm, RET_V_WIDTH), row),
        pl.BlockSpec((tm, 2 * d), lambda i: (i, 2)),
        pl.BlockSpec((tm, d), row),
        pl.BlockSpec((1, N_ADA, d), lambda i: (i // per, 0, 0)),
        pl.BlockSpec((None, d, d), lambda i: (layer, 0, 0)),
        pl.BlockSpec((None, RET_V_WIDTH, d), lambda i: (layer, 0, 0)),
        pl.BlockSpec((None, d, d), lambda i: (layer, 0, 0)),
        pl.BlockSpec((1, d), const),
        pl.BlockSpec((1, d), const),
    ]
    args = [uc, og, gates, xf, ada_l, wc, wr, wo, ln_g.reshape(1, d), ln_b.reshape(1, d)]
    out_specs = [pl.BlockSpec((tm, d), row), pl.BlockSpec((tm, d), row)]
    out_shape = [jax.ShapeDtypeStruct((n, d), F32), jax.ShapeDtypeStruct((n, d), F32 if router else BF16)]
    scratch = []
    if router:
        wrt = jnp.zeros((d, LANES), F32).at[:, :N_EXPERTS].set(w_router)
        in_specs.append(pl.BlockSpec((d, LANES), const))
        args.append(wrt)
        out_specs += [pl.BlockSpec((tm, LANES), row), pl.BlockSpec((1, LANES), const)]
        out_shape += [jax.ShapeDtypeStruct((n, LANES), F32), jax.ShapeDtypeStruct((1, LANES), F32)]
        scratch = [pltpu.VMEM((1, LANES), F32)]
    return pl.pallas_call(
        functools.partial(_mix_kernel, router=router),
        grid=(n // tm,),
        in_specs=in_specs,
        out_specs=out_specs,
        out_shape=out_shape,
        scratch_shapes=scratch,
        compiler_params=_params("arbitrary" if router else "parallel"),
        name="mix_out_router" if router else "mix_out",
    )(*args)


def _residual_epilogue(ff, x_ref, ada_ref, adan_ref, g_ref, b_ref, xo_ref, ho_ref):
    gate2 = ada_ref[0, 5:6, :]
    xn = _layer_norm_rows(DEEPNORM_ALPHA * x_ref[...] + gate2 * ff, g_ref[...], b_ref[...])
    xo_ref[...] = xn
    if ho_ref is not None:
        shift1 = adan_ref[0, 0:1, :]
        scale1 = adan_ref[0, 1:2, :]
        ho_ref[...] = (xn * (1.0 + scale1) + shift1).astype(BF16)


def _swiglu_cols(h, wg_ref, wu_ref, wd_ref, lo, hi):
    gate = _bdot(h, wg_ref[0, :, lo:hi])
    up = _bdot(h, wu_ref[0, :, lo:hi])
    return _bdot((gate * _sigmoid(gate) * up).astype(BF16), wd_ref[0, lo:hi, :])


def _ffn_kernel(*refs, emit_h):
    if emit_h:
        h_ref, wg_ref, wu_ref, wd_ref, x_ref, ada_ref, adan_ref, g_ref, b_ref, xo_ref, ho_ref, acc_ref = refs
    else:
        h_ref, wg_ref, wu_ref, wd_ref, x_ref, ada_ref, g_ref, b_ref, xo_ref, acc_ref = refs
        adan_ref = ho_ref = None
    c = pl.program_id(1)

    @pl.when(c == 0)
    def _():
        acc_ref[...] = jnp.zeros_like(acc_ref)

    acc_ref[...] += _swiglu_cols(h_ref[...], wg_ref, wu_ref, wd_ref, 0, wg_ref.shape[-1])

    @pl.when(c == pl.num_programs(1) - 1)
    def _():
        _residual_epilogue(acc_ref[...], x_ref, ada_ref, adan_ref, g_ref, b_ref, xo_ref, ho_ref)


def _ffn_call(h, wg, wu, wd, layer, xf, ada_l, ada_next, ln_g, ln_b, seq, tm, fc):
    n, d = xf.shape
    dff = wg.shape[-1]
    per = seq // tm
    emit_h = ada_next is not None
    row = lambda i, c: (i, 0)
    const = lambda i, c: (0, 0)
    ada_map = lambda i, c: (i // per, 0, 0)
    in_specs = [
        pl.BlockSpec((tm, d), row),
        pl.BlockSpec((1, d, fc), lambda i, c: (layer, 0, c)),
        pl.BlockSpec((1, d, fc), lambda i, c: (layer, 0, c)),
        pl.BlockSpec((1, fc, d), lambda i, c: (layer, c, 0)),
        pl.BlockSpec((tm, d), row),
        pl.BlockSpec((1, N_ADA, d), ada_map),
    ]
    args = [h, wg, wu, wd, xf, ada_l]
    if emit_h:
        in_specs.append(pl.BlockSpec((1, N_ADA, d), ada_map))
        args.append(ada_next)
    in_specs += [pl.BlockSpec((1, d), const), pl.BlockSpec((1, d), const)]
    args += [ln_g.reshape(1, d), ln_b.reshape(1, d)]
    out_specs = [pl.BlockSpec((tm, d), row)]
    out_shape = [jax.ShapeDtypeStruct((n, d), F32)]
    if emit_h:
        out_specs.append(pl.BlockSpec((tm, d), row))
        out_shape.append(jax.ShapeDtypeStruct((n, d), BF16))
    outs = pl.pallas_call(
        functools.partial(_ffn_kernel, emit_h=emit_h),
        grid=(n // tm, dff // fc),
        in_specs=in_specs,
        out_specs=out_specs,
        out_shape=out_shape,
        scratch_shapes=[pltpu.VMEM((tm, d), F32)],
        compiler_params=_params("parallel", "arbitrary"),
        name="dense_ffn",
    )(*args)
    return outs if emit_h else (outs[0], None)


def _row_copy(src, src_row, dst, dst_row, sem):
    return pltpu.make_async_copy(src.at[pl.ds(src_row, 1)], dst.at[pl.ds(dst_row, 1)], sem)


def _scatter_kernel(zt_ref, p1_ref, p2_ref, hf_ref, xs_hbm, zero_ref, sem):
    i = pl.program_id(0)
    tb = p1_ref.shape[0]
    tile = zero_ref.shape[0]

    def zero_copy(j):
        return pltpu.make_async_copy(zero_ref, xs_hbm.at[pl.ds(zt_ref[j] * tile, tile)], sem)

    @pl.when(i == 0)
    def _():
        zero_ref[...] = jnp.zeros_like(zero_ref)
        for j in range(zt_ref.shape[0]):
            pl.when(zt_ref[j] >= 0)(lambda j=j: zero_copy(j).start())
        for j in range(zt_ref.shape[0]):
            pl.when(zt_ref[j] >= 0)(lambda j=j: zero_copy(j).wait())

    def start(k, carry):
        _row_copy(hf_ref, k, xs_hbm, p1_ref[k], sem).start(priority=0)
        _row_copy(hf_ref, k, xs_hbm, p2_ref[k], sem).start(priority=1)
        return carry

    def wait(k, carry):
        _row_copy(hf_ref, k, xs_hbm, p1_ref[k], sem).wait()
        _row_copy(hf_ref, k, xs_hbm, p2_ref[k], sem).wait()
        return carry

    lax.fori_loop(0, tb, start, 0, unroll=8)
    lax.fori_loop(0, tb, wait, 0, unroll=8)


def _scatter_call(hf, p1, p2, zero_tiles, rows, tile, tb):
    n, d = hf.shape
    return pl.pallas_call(
        _scatter_kernel,
        grid_spec=pltpu.PrefetchScalarGridSpec(
            num_scalar_prefetch=1,
            grid=(n // tb,),
            in_specs=[
                pl.BlockSpec((tb,), lambda i, zt: (i,), memory_space=pltpu.SMEM),
                pl.BlockSpec((tb,), lambda i, zt: (i,), memory_space=pltpu.SMEM),
                pl.BlockSpec((tb, d), lambda i, zt: (i, 0)),
            ],
            out_specs=pl.BlockSpec(memory_space=pl.ANY),
            scratch_shapes=[pltpu.VMEM((tile, d), F32), pltpu.SemaphoreType.DMA(())],
        ),
        out_shape=jax.ShapeDtypeStruct((rows, d), F32),
        compiler_params=_params("arbitrary"),
        name="moe_scatter",
    )(zero_tiles, p1, p2, hf)


def _expert_kernel(te_ref, xs_ref, wg_ref, wu_ref, wd_ref, ys_ref):
    h = xs_ref[...].astype(BF16)
    half = wg_ref.shape[-1] // 2
    ys_ref[...] = (_swiglu_cols(h, wg_ref, wu_ref, wd_ref, 0, half)
                   + _swiglu_cols(h, wg_ref, wu_ref, wd_ref, half, 2 * half))


def _expert_call(xs, wg, wu, wd, tile_expert, tile):
    rows, d = xs.shape
    dff = wg.shape[-1]
    return pl.pallas_call(
        _expert_kernel,
        grid_spec=pltpu.PrefetchScalarGridSpec(
            num_scalar_prefetch=1,
            grid=(rows // tile,),
            in_specs=[
                pl.BlockSpec((tile, d), lambda r, te: (r, 0)),
                pl.BlockSpec((1, d, dff), lambda r, te: (te[r], 0, 0)),
                pl.BlockSpec((1, d, dff), lambda r, te: (te[r], 0, 0)),
                pl.BlockSpec((1, dff, d), lambda r, te: (te[r], 0, 0)),
            ],
            out_specs=pl.BlockSpec((tile, d), lambda r, te: (r, 0)),
        ),
        out_shape=jax.ShapeDtypeStruct((rows, d), F32),
        compiler_params=_params("arbitrary"),
        name="moe_experts",
    )(tile_expert, xs, wg, wu, wd)


def _combine_kernel(*refs, emit_h):
    if emit_h:
        (p1_ref, p2_ref, p1n_ref, p2n_ref, ys_hbm, rec_ref, x_ref, ada_ref, adan_ref, g_ref, b_ref,
         xo_ref, ho_ref, buf_ref, sem) = refs
    else:
        (p1_ref, p2_ref, p1n_ref, p2n_ref, ys_hbm, rec_ref, x_ref, ada_ref, g_ref, b_ref,
         xo_ref, buf_ref, sem) = refs
        adan_ref = ho_ref = None
    tm = x_ref.shape[0]
    i = pl.program_id(0)
    slot = i % 2

    def rows(pa_ref, pb_ref, s, k):
        return (_row_copy(ys_hbm, pa_ref[k], buf_ref.at[s, 0], k, sem.at[s]),
                _row_copy(ys_hbm, pb_ref[k], buf_ref.at[s, 1], k, sem.at[s]))

    def start_tile(pa_ref, pb_ref, s):
        def body(k, carry):
            a, b = rows(pa_ref, pb_ref, s, k)
            a.start(priority=0)
            b.start(priority=1)
            return carry
        lax.fori_loop(0, tm, body, 0, unroll=8)

    @pl.when(i == 0)
    def _():
        start_tile(p1_ref, p2_ref, 0)

    @pl.when(i + 1 < pl.num_programs(0))
    def _():
        start_tile(p1n_ref, p2n_ref, 1 - slot)

    def wait_body(k, carry):
        a, b = rows(p1_ref, p2_ref, slot, k)
        a.wait()
        b.wait()
        return carry

    lax.fori_loop(0, tm, wait_body, 0, unroll=8)
    w1 = rec_ref[:, R_W1:R_W1 + 1]
    w2 = rec_ref[:, R_W2:R_W2 + 1]
    ff = w1 * buf_ref[slot, 0] + w2 * buf_ref[slot, 1]
    _residual_epilogue(ff, x_ref, ada_ref, adan_ref, g_ref, b_ref, xo_ref, ho_ref)


def _combine_call(ys, p1, p2, rec, xf, ada_l, ada_next, ln_g, ln_b, seq, tm):
    n, d = xf.shape
    per = seq // tm
    emit_h = ada_next is not None
    row = lambda i: (i, 0)
    const = lambda i: (0, 0)
    ada_map = lambda i: (i // per, 0, 0)
    last = n // tm - 1
    cur = lambda i: (i,)
    nxt = lambda i: (jnp.minimum(i + 1, last),)
    in_specs = [
        pl.BlockSpec((tm,), cur, memory_space=pltpu.SMEM),
        pl.BlockSpec((tm,), cur, memory_space=pltpu.SMEM),
        pl.BlockSpec((tm,), nxt, memory_space=pltpu.SMEM),
        pl.BlockSpec((tm,), nxt, memory_space=pltpu.SMEM),
        pl.BlockSpec(memory_space=pl.ANY),
        pl.BlockSpec((tm, LANES), row),
        pl.BlockSpec((tm, d), row),
        pl.BlockSpec((1, N_ADA, d), ada_map),
    ]
    args = [p1, p2, p1, p2, ys, rec, xf, ada_l]
    if emit_h:
        in_specs.append(pl.BlockSpec((1, N_ADA, d), ada_map))
        args.append(ada_next)
    in_specs += [pl.BlockSpec((1, d), const), pl.BlockSpec((1, d), const)]
    args += [ln_g.reshape(1, d), ln_b.reshape(1, d)]
    out_specs = [pl.BlockSpec((tm, d), row)]
    out_shape = [jax.ShapeDtypeStruct((n, d), F32)]
    if emit_h:
        out_specs.append(pl.BlockSpec((tm, d), row))
        out_shape.append(jax.ShapeDtypeStruct((n, d), BF16))
    outs = pl.pallas_call(
        functools.partial(_combine_kernel, emit_h=emit_h),
        grid=(n // tm,),
        in_specs=in_specs,
        out_specs=out_specs,
        out_shape=out_shape,
        scratch_shapes=[pltpu.VMEM((2, 2, tm, d), F32), pltpu.SemaphoreType.DMA((2,))],
        compiler_params=_params("arbitrary"),
        name="moe_combine",
    )(*args)
    return outs if emit_h else (outs[0], None)


def _moe_call(hf, rec, counts, wg, wu, wd, expert_base, xf, ada_l, ada_next, ln_g, ln_b, seq, tile, tm):
    n = hf.shape[0]
    total_tiles = 2 * n // tile + N_EXPERTS
    cnt = counts[0, :N_EXPERTS].astype(jnp.int32)
    ntiles = (cnt + tile - 1) // tile
    end_tile = jnp.cumsum(ntiles)
    start_row = (end_tile - ntiles) * tile
    p1 = start_row[rec[:, R_E1].astype(jnp.int32)] + rec[:, R_RANK1].astype(jnp.int32)
    p2 = start_row[rec[:, R_E2].astype(jnp.int32)] + rec[:, R_RANK2].astype(jnp.int32)
    tile_ids = jnp.arange(total_tiles, dtype=jnp.int32)
    tile_expert = jnp.minimum(jnp.sum(tile_ids[:, None] >= end_tile[None, :], axis=1), N_EXPERTS - 1)
    partial_last = jnp.where(cnt % tile != 0, end_tile - 1, -1)
    tail = end_tile[-1] + jnp.arange(N_EXPERTS, dtype=jnp.int32)
    zero_tiles = jnp.concatenate([partial_last, jnp.where(tail < total_tiles, tail, -1)]).astype(jnp.int32)

    xs = _scatter_call(hf, p1, p2, zero_tiles, total_tiles * tile, tile, tm)
    ys = _expert_call(xs, wg, wu, wd, (tile_expert + expert_base).astype(jnp.int32), tile)
    return _combine_call(ys, p1, p2, rec, xf, ada_l, ada_next, ln_g, ln_b, seq, tm)


def kernel(x, c, positions, w_ada, b_ada, w_in, w_dw, b_dw, ln_conv_g, ln_conv_b, w_conv_o, w_ret_o, w_out, ln1_g, ln1_b, ffn_w_gate, ffn_w_up, ffn_w_down, moe_w_router, moe_w_gate, moe_w_up, moe_w_down, ln2_g, ln2_b):
    bsz, seq, d = x.shape
    n = bsz * seq
    tm = 512
    tm_proj = 2048
    tm_mix = 256
    fc = D_FF // 2
    tt = 256
    ret_chunk = 256
    moe_tile = 256

    w_in_b = _cast_call(w_in)
    w_conv_o_b = _cast_call(w_conv_o)
    w_ret_o_b = _cast_call(w_ret_o)
    w_out_b = _cast_call(w_out)
    ffn_b = [_cast_call(w) for w in (ffn_w_gate, ffn_w_up, ffn_w_down)]
    moe_b = [_cast_call(w.reshape((-1,) + w.shape[2:])) for w in (moe_w_gate, moe_w_up, moe_w_down)]

    xf = x.reshape(n, d)
    ada = _ada_call(c, w_ada, b_ada)
    cos, sin = _rope_call(positions, tm)
    h = _modulate_call(xf, ada[0], seq, tm)

    for l in range(DEPTH):
        uc = _conv_call(h, w_in_b, l, w_dw[l], b_dw[l], ln_conv_g[l], ln_conv_b[l], bsz, seq, tt)
        qk = _qk_call(h, w_in_b, l, cos, sin, tm_proj)
        vgg = _vgg_call(h, w_in_b, l, tm_proj)
        og = _ret_call(qk, vgg, bsz, seq, ret_chunk)

        i = l // 2
        is_moe = l % 2 == 1
        mixed = _mix_call(
            uc.reshape(n, d), og, vgg, xf, ada[l], w_conv_o_b, w_ret_o_b, w_out_b, l,
            ln1_g[l], ln1_b[l], moe_w_router[i] if is_moe else None, seq, tm_mix)
        ada_next = ada[l + 1] if l + 1 < DEPTH else None
        if is_moe:
            xf, hf, rec, counts = mixed
            xf, h = _moe_call(hf, rec, counts, *moe_b, i * N_EXPERTS, xf, ada[l], ada_next,
                              ln2_g[l], ln2_b[l], seq, moe_tile, tm)
        else:
            xf, h2 = mixed
            xf, h = _ffn_call(h2, *ffn_b, i, xf, ada[l], ada_next, ln2_g[l], ln2_b[l], seq, tm, fc)
    return xf.reshape(bsz, seq, d)
```

```python
import functools
import math

import jax
import jax.numpy as jnp
from jax import lax
from jax.experimental import pallas as pl
from jax.experimental.pallas import tpu as pltpu

D_MODEL = 1024
DEPTH = 4
CONV_KERNEL = 31
RET_QK_DIM = 256
RET_V_DIM = 512
RET_HEADS = 4
RET_V_WIDTH = RET_HEADS * RET_V_DIM
RET_CHUNK = 128
ROPE_BASE = 10000.0
D_FF = 2816
N_EXPERTS = 8
N_ADA = 6
DEEPNORM_ALPHA = (2 * DEPTH) ** 0.25
LN_EPS = 1e-5

LANES = 128
SUBLANES = 8
CONV_HALO = 32
CONV_ROWS = 128
NORM_ROWS = 32
CAST_ROWS = 512
CAST_MAX_COLS = D_FF
VMEM_LIMIT = 56 * 1024 * 1024

R_E1, R_E2, R_RANK1, R_RANK2, R_W1, R_W2 = range(6)

F32 = jnp.float32
BF16 = jnp.bfloat16


def _params(*sem):
    return pltpu.CompilerParams(dimension_semantics=sem, vmem_limit_bytes=VMEM_LIMIT)


def _bdot(a, b):
    return jnp.dot(a, b, preferred_element_type=F32)


def _split_bf16(a):
    hi = a.astype(BF16)
    lo = (a - hi.astype(F32)).astype(BF16)
    return hi, lo


def _dot3(a, b):
    a_hi, a_lo = _split_bf16(a)
    b_hi, b_lo = _split_bf16(b)
    return _bdot(a_hi, b_hi) + (_bdot(a_hi, b_lo) + _bdot(a_lo, b_hi))


def _sigmoid(x):
    return jax.nn.sigmoid(x)


def _layer_norm_rows(y, g, b):
    mu = jnp.mean(y, axis=-1, keepdims=True)
    d = y - mu
    var = jnp.mean(d * d, axis=-1, keepdims=True)
    return d * lax.rsqrt(var + LN_EPS) * g + b


def _cast_kernel(w_ref, o_ref):
    o_ref[...] = w_ref[...].astype(BF16)


def _cast_call(w):
    cols = w.shape[-1]
    w2 = w.reshape(-1, cols)
    rows = w2.shape[0]
    bc = cols
    while bc > CAST_MAX_COLS:
        bc //= 2
    assert rows % CAST_ROWS == 0 and cols % bc == 0 and bc % LANES == 0, (w.shape, bc)
    out = pl.pallas_call(
        _cast_kernel,
        grid=(rows // CAST_ROWS, cols // bc),
        in_specs=[pl.BlockSpec((CAST_ROWS, bc), lambda i, j: (i, j))],
        out_specs=pl.BlockSpec((CAST_ROWS, bc), lambda i, j: (i, j)),
        out_shape=jax.ShapeDtypeStruct((rows, cols), BF16),
        compiler_params=_params("parallel", "parallel"),
        name="cast_bf16",
    )(w2)
    return out.reshape(w.shape)


def _ada_kernel(c_ref, w_ref, b_ref, o_ref):
    c = c_ref[...]
    o_ref[0] = _dot3(c * _sigmoid(c), w_ref[0]) + b_ref[0]


def _ada_call(c, w_ada, b_ada):
    depth, d, width = w_ada.shape
    bsz = c.shape[0]
    ncol = width // d
    out = pl.pallas_call(
        _ada_kernel,
        grid=(depth, ncol),
        in_specs=[
            pl.BlockSpec((bsz, d), lambda l, j: (0, 0)),
            pl.BlockSpec((1, d, d), lambda l, j: (l, 0, j)),
            pl.BlockSpec((1, 1, d), lambda l, j: (l, 0, j)),
        ],
        out_specs=pl.BlockSpec((1, bsz, d), lambda l, j: (l, 0, j)),
        out_shape=jax.ShapeDtypeStruct((depth, bsz, width), F32),
        compiler_params=_params("parallel", "parallel"),
        name="ada",
    )(c, w_ada, b_ada.reshape(depth, 1, width))
    return out.reshape(depth, bsz, N_ADA, d)


def _rope_kernel(pos_ref, inv_ref, cos_ref, sin_ref):
    ang = pos_ref[...].astype(F32) * inv_ref[...]
    cos_ref[...] = jnp.cos(ang)
    sin_ref[...] = jnp.sin(ang)


def _rope_call(positions, tm):
    n = positions.size
    half = RET_QK_DIM // 2
    inv_freq = ROPE_BASE ** (-jnp.arange(half, dtype=F32) / half)
    return pl.pallas_call(
        _rope_kernel,
        grid=(n // tm,),
        in_specs=[
            pl.BlockSpec((tm, 1), lambda i: (i, 0)),
            pl.BlockSpec((1, half), lambda i: (0, 0)),
        ],
        out_specs=[pl.BlockSpec((tm, half), lambda i: (i, 0))] * 2,
        out_shape=[jax.ShapeDtypeStruct((n, half), F32)] * 2,
        compiler_params=_params("parallel"),
        name="rope",
    )(positions.reshape(n, 1), inv_freq.reshape(1, half))


def _modulate_kernel(x_ref, ada_ref, h_ref):
    shift = ada_ref[0, 0:1, :]
    scale = ada_ref[0, 1:2, :]
    h_ref[...] = (x_ref[...] * (1.0 + scale) + shift).astype(BF16)


def _modulate_call(xf, ada_l, seq, tm):
    n, d = xf.shape
    per = seq // tm
    return pl.pallas_call(
        _modulate_kernel,
        grid=(n // tm,),
        in_specs=[
            pl.BlockSpec((tm, d), lambda i: (i, 0)),
            pl.BlockSpec((1, N_ADA, d), lambda i: (i // per, 0, 0)),
        ],
        out_specs=pl.BlockSpec((tm, d), lambda i: (i, 0)),
        out_shape=jax.ShapeDtypeStruct((n, d), BF16),
        compiler_params=_params("parallel"),
        name="modulate",
    )(xf, ada_l)


def _qk_kernel(h_ref, w_ref, cos_ref, sin_ref, o_ref):
    acc = _bdot(h_ref[...], w_ref[...])
    scale = jnp.where(pl.program_id(0) == 0, 1.0, RET_QK_DIM ** -0.5).astype(F32)
    cos = cos_ref[...] * scale
    sin = sin_ref[...] * scale
    half = RET_QK_DIM // 2
    for hd in range(RET_HEADS):
        lo = hd * RET_QK_DIM
        x1 = acc[:, lo:lo + half]
        x2 = acc[:, lo + half:lo + RET_QK_DIM]
        o_ref[0, :, lo:lo + half] = (x1 * cos - x2 * sin).astype(BF16)
        o_ref[0, :, lo + half:lo + RET_QK_DIM] = (x2 * cos + x1 * sin).astype(BF16)


def _qk_call(h, w_in_b, layer, cos, sin, tm):
    n, d = h.shape
    half = RET_QK_DIM // 2
    return pl.pallas_call(
        _qk_kernel,
        grid=(2, n // tm),
        in_specs=[
            pl.BlockSpec((tm, d), lambda j, i: (i, 0)),
            pl.BlockSpec((None, d, d), lambda j, i: (layer, 0, 2 + j)),
            pl.BlockSpec((tm, half), lambda j, i: (i, 0)),
            pl.BlockSpec((tm, half), lambda j, i: (i, 0)),
        ],
        out_specs=pl.BlockSpec((1, tm, d), lambda j, i: (j, i, 0)),
        out_shape=jax.ShapeDtypeStruct((2, n, d), BF16),
        compiler_params=_params("parallel", "parallel"),
        name="proj_qk",
    )(h, w_in_b, cos, sin)


def _vgg_kernel(h_ref, w_ref, o_ref):
    j = pl.program_id(0)

    @pl.when(j < 2)
    def _():
        o_ref[...] = _bdot(h_ref[...], w_ref[...]).astype(BF16)

    @pl.when(jnp.logical_and(j >= 2, j < 4))
    def _():
        acc = _bdot(h_ref[...], w_ref[...])
        o_ref[...] = (acc * _sigmoid(acc)).astype(BF16)

    @pl.when(j >= 4)
    def _():
        o_ref[...] = _sigmoid(_bdot(h_ref[...], w_ref[...])).astype(BF16)


def _vgg_call(h, w_in_b, layer, tm):
    n, d = h.shape
    col0, ncols = 4, 6
    return pl.pallas_call(
        _vgg_kernel,
        grid=(ncols, n // tm),
        in_specs=[
            pl.BlockSpec((tm, d), lambda j, i: (i, 0)),
            pl.BlockSpec((None, d, d), lambda j, i: (layer, 0, col0 + j)),
        ],
        out_specs=pl.BlockSpec((tm, d), lambda j, i: (i, j)),
        out_shape=jax.ShapeDtypeStruct((n, ncols * d), BF16),
        compiler_params=_params("parallel", "parallel"),
        name="proj_v_gates",
    )(h, w_in_b)


def _conv_kernel(h_ref, wa_ref, wb_ref, w_ref, bias_ref, g_ref, b_ref, o_ref, u_ref, buf_ref, y_ref, *, tt):
    ch = o_ref.shape[-1]
    t = pl.program_id(1)

    @pl.when(t == 0)
    def _():
        u_ref[...] = jnp.zeros_like(u_ref)
        buf_ref[0, tt:, :] = jnp.zeros((CONV_HALO, ch), F32)

    buf_ref[0, 0:CONV_HALO, :] = jnp.where(t <= 1, 0.0, buf_ref[0, tt:, :])
    buf_ref[0, CONV_HALO:, :] = u_ref[...]

    h = h_ref[...]
    u_ref[...] = _bdot(h, wa_ref[...]) * _sigmoid(_bdot(h, wb_ref[...]))

    sub = lax.broadcasted_iota(jnp.int32, (SUBLANES, ch), 0)
    n_shift_tiles = (tt + CONV_HALO) // SUBLANES - 1
    for s in range(1, SUBLANES):
        own = sub < SUBLANES - s
        cur = pltpu.roll(buf_ref[0, 0:SUBLANES, :], SUBLANES - s, 0)
        for j in range(n_shift_tiles):
            nxt = pltpu.roll(buf_ref[0, (j + 1) * SUBLANES:(j + 2) * SUBLANES, :], SUBLANES - s, 0)
            buf_ref[s, j * SUBLANES:(j + 1) * SUBLANES, :] = jnp.where(own, cur, nxt)
            cur = nxt

    base = CONV_HALO - (CONV_KERNEL - 1)
    for lg in range(ch // LANES):
        ls = slice(lg * LANES, (lg + 1) * LANES)
        bias = bias_ref[:, ls]
        for r0 in range(0, tt, CONV_ROWS):
            acc = jnp.zeros((CONV_ROWS, LANES), F32)
            for k in range(CONV_KERNEL):
                a, s = divmod(base + k, SUBLANES)
                lo = r0 + a * SUBLANES
                acc = acc + buf_ref[s, lo:lo + CONV_ROWS, ls] * w_ref[k:k + 1, ls]
            y_ref[r0:r0 + CONV_ROWS, ls] = acc + bias

    g = g_ref[...]
    b = b_ref[...]
    for r0 in range(0, tt, NORM_ROWS):
        y = _layer_norm_rows(y_ref[r0:r0 + NORM_ROWS, :], g, b)
        o_ref[0, r0:r0 + NORM_ROWS, :] = (y * _sigmoid(y)).astype(BF16)


def _conv_call(h, w_in_b, layer, w_dw, b_dw, ln_g, ln_b, bsz, seq, tt):
    n, d = h.shape
    ch = w_dw.shape[-1]
    nt = seq // tt
    const = lambda b, t: (0, 0)
    return pl.pallas_call(
        functools.partial(_conv_kernel, tt=tt),
        grid=(bsz, nt + 1),
        in_specs=[
            pl.BlockSpec((tt, d), lambda b, t: (b * nt + jnp.minimum(t, nt - 1), 0)),
            pl.BlockSpec((None, d, ch), lambda b, t: (layer, 0, 0)),
            pl.BlockSpec((None, d, ch), lambda b, t: (layer, 0, 1)),
            pl.BlockSpec((CONV_KERNEL, ch), const),
            pl.BlockSpec((1, ch), const),
            pl.BlockSpec((1, ch), const),
            pl.BlockSpec((1, ch), const),
        ],
        out_specs=pl.BlockSpec((1, tt, ch), lambda b, t: (b, jnp.maximum(t - 1, 0), 0)),
        out_shape=jax.ShapeDtypeStruct((bsz, seq, ch), BF16),
        scratch_shapes=[pltpu.VMEM((tt, ch), F32),
                        pltpu.VMEM((SUBLANES, CONV_HALO + tt, ch), F32),
                        pltpu.VMEM((tt, ch), F32)],
        compiler_params=_params("parallel", "arbitrary"),
        name="conv_glu_ln",
    )(h, w_in_b, w_in_b, w_dw, b_dw.reshape(1, ch), ln_g.reshape(1, ch), ln_b.reshape(1, ch))


def _ret_kernel(q_ref, k_ref, v_ref, g_ref, o_ref, s_ref, intra_ref, *, chunk):
    log_gamma = [math.log1p(-(2.0 ** (-5 - hd))) for hd in range(RET_HEADS)]

    @pl.when(pl.program_id(1) == 0)
    def _():
        s_ref[...] = jnp.zeros_like(s_ref)
        diff = (lax.broadcasted_iota(jnp.int32, (chunk, chunk), 0)
                - lax.broadcasted_iota(jnp.int32, (chunk, chunk), 1)).astype(F32)
        for hd in range(RET_HEADS):
            intra_ref[hd] = jnp.where(diff >= 0, jnp.exp(log_gamma[hd] * jnp.maximum(diff, 0.0)), 0.0)

    row = lax.broadcasted_iota(jnp.int32, (chunk, 1), 0).astype(F32)
    for hd in range(RET_HEADS):
        log_g = log_gamma[hd]
        q_decay = jnp.exp(log_g * (row + 1.0))
        k_decay = jnp.exp(log_g * (chunk - 1.0 - row))
        chunk_decay = math.exp(log_g * chunk)
        intra = intra_ref[hd]

        qs = slice(hd * RET_QK_DIM, (hd + 1) * RET_QK_DIM)
        vs = slice(hd * RET_V_DIM, (hd + 1) * RET_V_DIM)
        q = q_ref[:, qs]
        k = k_ref[:, qs]
        v = v_ref[:, vs]
        state = s_ref[hd]
        scores = lax.dot_general(q, k, (((1,), (1,)), ((), ())), preferred_element_type=F32)
        inner = _bdot((scores * intra).astype(BF16), v)
        cross = _bdot(q, state.astype(BF16)) * q_decay
        update = lax.dot_general((k.astype(F32) * k_decay).astype(BF16), v, (((0,), (0,)), ((), ())),
                                 preferred_element_type=F32)
        s_ref[hd] = state * chunk_decay + update

        o = inner + cross
        mu = jnp.mean(o, axis=-1, keepdims=True)
        d = o - mu
        var = jnp.mean(d * d, axis=-1, keepdims=True)
        o_ref[:, vs] = (g_ref[:, vs] * (d * lax.rsqrt(var + LN_EPS))).astype(BF16)


def _ret_call(qk, vgg, bsz, seq, chunk):
    width = RET_HEADS * RET_QK_DIM
    qk4 = qk.reshape(2, bsz, seq, width)
    v3 = g3 = vgg.reshape(bsz, seq, vgg.shape[-1])
    out = pl.pallas_call(
        functools.partial(_ret_kernel, chunk=chunk),
        grid=(bsz, seq // chunk),
        in_specs=[
            pl.BlockSpec((None, None, chunk, width), lambda b, n: (0, b, n, 0)),
            pl.BlockSpec((None, None, chunk, width), lambda b, n: (1, b, n, 0)),
            pl.BlockSpec((None, chunk, RET_V_WIDTH), lambda b, n: (b, n, 0)),
            pl.BlockSpec((None, chunk, RET_V_WIDTH), lambda b, n: (b, n, 1)),
        ],
        out_specs=pl.BlockSpec((None, chunk, RET_V_WIDTH), lambda b, n: (b, n, 0)),
        out_shape=jax.ShapeDtypeStruct((bsz, seq, RET_V_WIDTH), BF16),
        scratch_shapes=[pltpu.VMEM((RET_HEADS, RET_QK_DIM, RET_V_DIM), F32),
                        pltpu.VMEM((RET_HEADS, chunk, chunk), F32)],
        compiler_params=_params("parallel", "arbitrary"),
        name="retention",
    )(qk4, qk4, v3, g3)
    return out.reshape(bsz * seq, RET_V_WIDTH)


def _route(logits, carry_ref):
    rows = logits.shape[0]
    lane = lax.broadcasted_iota(jnp.int32, logits.shape, 1)
    neg = jnp.float32(-jnp.inf)
    lg = jnp.where(lane < N_EXPERTS, logits, neg)
    m1 = jnp.max(lg, axis=-1, keepdims=True)
    i1 = jnp.min(jnp.where(lg == m1, lane, LANES), axis=-1, keepdims=True)
    lg2 = jnp.where(lane == i1, neg, lg)
    m2 = jnp.max(lg2, axis=-1, keepdims=True)
    i2 = jnp.min(jnp.where(lg2 == m2, lane, LANES), axis=-1, keepdims=True)
    e = jnp.exp(m2 - m1)
    w1 = 1.0 / (1.0 + e)
    w2 = e / (1.0 + e)

    sel = jnp.where(jnp.logical_or(lane == i1, lane == i2), 1.0, 0.0)
    tri = jnp.where(lax.broadcasted_iota(jnp.int32, (rows, rows), 0)
                    >= lax.broadcasted_iota(jnp.int32, (rows, rows), 1), 1.0, 0.0).astype(BF16)
    incl = _bdot(tri, sel.astype(BF16))
    rank = carry_ref[...] + incl - sel
    carry_ref[...] = carry_ref[...] + incl[rows - 1:rows, :]
    r1 = jnp.sum(jnp.where(lane == i1, rank, 0.0), axis=-1, keepdims=True)
    r2 = jnp.sum(jnp.where(lane == i2, rank, 0.0), axis=-1, keepdims=True)

    rec = jnp.zeros(logits.shape, F32)
    for idx, val in ((R_E1, i1.astype(F32)), (R_E2, i2.astype(F32)), (R_RANK1, r1), (R_RANK2, r2),
                     (R_W1, w1), (R_W2, w2)):
        rec = jnp.where(lane == idx, val, rec)
    return rec


def _mix_kernel(*refs, router):
    if router:
        (uc_ref, og_ref, gt_ref, x_ref, ada_ref, wc_ref, wr_ref, wo_ref, g_ref, b_ref, wrt_ref,
         xo_ref, ho_ref, rec_ref, cnt_ref, carry_ref) = refs
    else:
        (uc_ref, og_ref, gt_ref, x_ref, ada_ref, wc_ref, wr_ref, wo_ref, g_ref, b_ref,
         xo_ref, ho_ref) = refs
    d = x_ref.shape[-1]
    y_conv = _bdot(uc_ref[...], wc_ref[...])
    y_ret = _bdot(og_ref[...], wr_ref[...])
    merged = gt_ref[:, 0:d] * y_conv + gt_ref[:, d:2 * d] * y_ret
    mix = _bdot(merged.astype(BF16), wo_ref[...])
    gate1 = ada_ref[0, 2:3, :]
    shift2 = ada_ref[0, 3:4, :]
    scale2 = ada_ref[0, 4:5, :]
    xn = _layer_norm_rows(DEEPNORM_ALPHA * x_ref[...] + gate1 * mix, g_ref[...], b_ref[...])
    xo_ref[...] = xn
    h2 = xn * (1.0 + scale2) + shift2
    ho_ref[...] = h2.astype(ho_ref.dtype)
    if router:
        @pl.when(pl.program_id(0) == 0)
        def _():
            carry_ref[...] = jnp.zeros_like(carry_ref)

        rec_ref[...] = _route(_dot3(h2, wrt_ref[...]), carry_ref)
        cnt_ref[...] = carry_ref[...]


def _mix_call(uc, og, gates, xf, ada_l, wc, wr, wo, layer, ln_g, ln_b, w_router, seq, tm):
    n, d = xf.shape
    per = seq // tm
    router = w_router is not None
    row = lambda i: (i, 0)
    const = lambda i: (0, 0)
    in_specs = [
        pl.BlockSpec((tm, d), row),
        pl.BlockSpec((tm, RET_V_WIDTH), row),
        pl.BlockSpec((tm, 2 * d), lambda i: (i, 2)),
        pl.BlockSpec((tm, d), row),
        pl.BlockSpec((1, N_ADA, d), lambda i: (i // per, 0, 0)),
        pl.BlockSpec((None, d, d), lambda i: (layer, 0, 0), pipeline_mode=pl.Buffered(1)),
        pl.BlockSpec((None, RET_V_WIDTH, d), lambda i: (layer, 0, 0), pipeline_mode=pl.Buffered(1)),
        pl.BlockSpec((None, d, d), lambda i: (layer, 0, 0), pipeline_mode=pl.Buffered(1)),
        pl.BlockSpec((1, d), const),
        pl.BlockSpec((1, d), const),
    ]
    args = [uc, og, gates, xf, ada_l, wc, wr, wo, ln_g.reshape(1, d), ln_b.reshape(1, d)]
    out_specs = [pl.BlockSpec((tm, d), row), pl.BlockSpec((tm, d), row)]
    out_shape = [jax.ShapeDtypeStruct((n, d), F32), jax.ShapeDtypeStruct((n, d), F32 if router else BF16)]
    scratch = []
    if router:
        wrt = jnp.zeros((d, LANES), F32).at[:, :N_EXPERTS].set(w_router)
        in_specs.append(pl.BlockSpec((d, LANES), const))
        args.append(wrt)
        out_specs += [pl.BlockSpec((tm, LANES), row), pl.BlockSpec((1, LANES), const)]
        out_shape += [jax.ShapeDtypeStruct((n, LANES), F32), jax.ShapeDtypeStruct((1, LANES), F32)]
        scratch = [pltpu.VMEM((1, LANES), F32)]
    return pl.pallas_call(
        functools.partial(_mix_kernel, router=router),
        grid=(n // tm,),
        in_specs=in_specs,
        out_specs=out_specs,
        out_shape=out_shape,
        scratch_shapes=scratch,
        compiler_params=_params("arbitrary" if router else "parallel"),
        name="mix_out_router" if router else "mix_out",
    )(*args)


def _residual_epilogue(ff, x_ref, ada_ref, adan_ref, g_ref, b_ref, xo_ref, ho_ref):
    gate2 = ada_ref[0, 5:6, :]
    xn = _layer_norm_rows(DEEPNORM_ALPHA * x_ref[...] + gate2 * ff, g_ref[...], b_ref[...])
    xo_ref[...] = xn
    if ho_ref is not None:
        shift1 = adan_ref[0, 0:1, :]
        scale1 = adan_ref[0, 1:2, :]
        ho_ref[...] = (xn * (1.0 + scale1) + shift1).astype(BF16)


def _swiglu_cols(h, wg_ref, wu_ref, wd_ref, lo, hi):
    gate = _bdot(h, wg_ref[0, :, lo:hi])
    up = _bdot(h, wu_ref[0, :, lo:hi])
    return _bdot((gate * _sigmoid(gate) * up).astype(BF16), wd_ref[0, lo:hi, :])


def _ffn_kernel(*refs, emit_h):
    if emit_h:
        h_ref, wg_ref, wu_ref, wd_ref, x_ref, ada_ref, adan_ref, g_ref, b_ref, xo_ref, ho_ref = refs
    else:
        h_ref, wg_ref, wu_ref, wd_ref, x_ref, ada_ref, g_ref, b_ref, xo_ref = refs
        adan_ref = ho_ref = None
    h = h_ref[...]
    half = wg_ref.shape[-1] // 2
    ff = (_swiglu_cols(h, wg_ref, wu_ref, wd_ref, 0, half)
          + _swiglu_cols(h, wg_ref, wu_ref, wd_ref, half, 2 * half))
    _residual_epilogue(ff, x_ref, ada_ref, adan_ref, g_ref, b_ref, xo_ref, ho_ref)


def _ffn_call(h, wg, wu, wd, layer, xf, ada_l, ada_next, ln_g, ln_b, seq, tm):
    n, d = xf.shape
    dff = wg.shape[-1]
    per = seq // tm
    emit_h = ada_next is not None
    row = lambda i: (i, 0)
    const = lambda i: (0, 0)
    ada_map = lambda i: (i // per, 0, 0)
    in_specs = [
        pl.BlockSpec((tm, d), row),
        pl.BlockSpec((1, d, dff), lambda i: (layer, 0, 0)),
        pl.BlockSpec((1, d, dff), lambda i: (layer, 0, 0)),
        pl.BlockSpec((1, dff, d), lambda i: (layer, 0, 0)),
        pl.BlockSpec((tm, d), row),
        pl.BlockSpec((1, N_ADA, d), ada_map),
    ]
    args = [h, wg, wu, wd, xf, ada_l]
    if emit_h:
        in_specs.append(pl.BlockSpec((1, N_ADA, d), ada_map))
        args.append(ada_next)
    in_specs += [pl.BlockSpec((1, d), const), pl.BlockSpec((1, d), const)]
    args += [ln_g.reshape(1, d), ln_b.reshape(1, d)]
    out_specs = [pl.BlockSpec((tm, d), row)]
    out_shape = [jax.ShapeDtypeStruct((n, d), F32)]
    if emit_h:
        out_specs.append(pl.BlockSpec((tm, d), row))
        out_shape.append(jax.ShapeDtypeStruct((n, d), BF16))
    outs = pl.pallas_call(
        functools.partial(_ffn_kernel, emit_h=emit_h),
        grid=(n // tm,),
        in_specs=in_specs,
        out_specs=out_specs,
        out_shape=out_shape,
        compiler_params=_params("parallel"),
        name="dense_ffn",
    )(*args)
    return outs if emit_h else (outs[0], None)


def _row_copy(src, src_row, dst, dst_row, sem):
    return pltpu.make_async_copy(src.at[pl.ds(src_row, 1)], dst.at[pl.ds(dst_row, 1)], sem)


def _scatter_kernel(zt_ref, p1_ref, p2_ref, hf_ref, xs_hbm, zero_ref, sem):
    i = pl.program_id(0)
    tb = p1_ref.shape[0]
    tile = zero_ref.shape[0]

    def zero_copy(j):
        return pltpu.make_async_copy(zero_ref, xs_hbm.at[pl.ds(zt_ref[j] * tile, tile)], sem)

    @pl.when(i == 0)
    def _():
        zero_ref[...] = jnp.zeros_like(zero_ref)
        for j in range(zt_ref.shape[0]):
            pl.when(zt_ref[j] >= 0)(lambda j=j: zero_copy(j).start())
        for j in range(zt_ref.shape[0]):
            pl.when(zt_ref[j] >= 0)(lambda j=j: zero_copy(j).wait())

    def start(k, carry):
        _row_copy(hf_ref, k, xs_hbm, p1_ref[k], sem).start(priority=0)
        _row_copy(hf_ref, k, xs_hbm, p2_ref[k], sem).start(priority=1)
        return carry

    def wait(k, carry):
        _row_copy(hf_ref, k, xs_hbm, p1_ref[k], sem).wait()
        _row_copy(hf_ref, k, xs_hbm, p2_ref[k], sem).wait()
        return carry

    lax.fori_loop(0, tb, start, 0, unroll=8)
    lax.fori_loop(0, tb, wait, 0, unroll=8)


def _scatter_call(hf, p1, p2, zero_tiles, rows, tile, tb):
    n, d = hf.shape
    return pl.pallas_call(
        _scatter_kernel,
        grid_spec=pltpu.PrefetchScalarGridSpec(
            num_scalar_prefetch=1,
            grid=(n // tb,),
            in_specs=[
                pl.BlockSpec((tb,), lambda i, zt: (i,), memory_space=pltpu.SMEM),
                pl.BlockSpec((tb,), lambda i, zt: (i,), memory_space=pltpu.SMEM),
                pl.BlockSpec((tb, d), lambda i, zt: (i, 0)),
            ],
            out_specs=pl.BlockSpec(memory_space=pl.ANY),
            scratch_shapes=[pltpu.VMEM((tile, d), F32), pltpu.SemaphoreType.DMA(())],
        ),
        out_shape=jax.ShapeDtypeStruct((rows, d), F32),
        compiler_params=_params("arbitrary"),
        name="moe_scatter",
    )(zero_tiles, p1, p2, hf)


def _expert_kernel(te_ref, xs_ref, wg_ref, wu_ref, wd_ref, ys_ref):
    h = xs_ref[...].astype(BF16)
    half = wg_ref.shape[-1] // 2
    ys_ref[...] = (_swiglu_cols(h, wg_ref, wu_ref, wd_ref, 0, half)
                   + _swiglu_cols(h, wg_ref, wu_ref, wd_ref, half, 2 * half))


def _expert_call(xs, wg, wu, wd, tile_expert, tile):
    rows, d = xs.shape
    dff = wg.shape[-1]
    return pl.pallas_call(
        _expert_kernel,
        grid_spec=pltpu.PrefetchScalarGridSpec(
            num_scalar_prefetch=1,
            grid=(rows // tile,),
            in_specs=[
                pl.BlockSpec((tile, d), lambda r, te: (r, 0)),
                pl.BlockSpec((1, d, dff), lambda r, te: (te[r], 0, 0)),
                pl.BlockSpec((1, d, dff), lambda r, te: (te[r], 0, 0)),
                pl.BlockSpec((1, dff, d), lambda r, te: (te[r], 0, 0)),
            ],
            out_specs=pl.BlockSpec((tile, d), lambda r, te: (r, 0)),
        ),
        out_shape=jax.ShapeDtypeStruct((rows, d), F32),
        compiler_params=_params("arbitrary"),
        name="moe_experts",
    )(tile_expert, xs, wg, wu, wd)


def _combine_kernel(*refs, emit_h):
    if emit_h:
        (p1_ref, p2_ref, p1n_ref, p2n_ref, ys_hbm, rec_ref, x_ref, ada_ref, adan_ref, g_ref, b_ref,
         xo_ref, ho_ref, buf_ref, sem) = refs
    else:
        (p1_ref, p2_ref, p1n_ref, p2n_ref, ys_hbm, rec_ref, x_ref, ada_ref, g_ref, b_ref,
         xo_ref, buf_ref, sem) = refs
        adan_ref = ho_ref = None
    tm = x_ref.shape[0]
    i = pl.program_id(0)
    slot = i % 2

    def rows(pa_ref, pb_ref, s, k):
        return (_row_copy(ys_hbm, pa_ref[k], buf_ref.at[s, 0], k, sem.at[s]),
                _row_copy(ys_hbm, pb_ref[k], buf_ref.at[s, 1], k, sem.at[s]))

    def start_tile(pa_ref, pb_ref, s):
        def body(k, carry):
            a, b = rows(pa_ref, pb_ref, s, k)
            a.start(priority=0)
            b.start(priority=1)
            return carry
        lax.fori_loop(0, tm, body, 0, unroll=8)

    @pl.when(i == 0)
    def _():
        start_tile(p1_ref, p2_ref, 0)

    @pl.when(i + 1 < pl.num_programs(0))
    def _():
        start_tile(p1n_ref, p2n_ref, 1 - slot)

    def wait_body(k, carry):
        a, b = rows(p1_ref, p2_ref, slot, k)
        a.wait()
        b.wait()
        return carry

    lax.fori_loop(0, tm, wait_body, 0, unroll=8)
    w1 = rec_ref[:, R_W1:R_W1 + 1]
    w2 = rec_ref[:, R_W2:R_W2 + 1]
    ff = w1 * buf_ref[slot, 0] + w2 * buf_ref[slot, 1]
    _residual_epilogue(ff, x_ref, ada_ref, adan_ref, g_ref, b_ref, xo_ref, ho_ref)


def _combine_call(ys, p1, p2, rec, xf, ada_l, ada_next, ln_g, ln_b, seq, tm):
    n, d = xf.shape
    per = seq // tm
    emit_h = ada_next is not None
    row = lambda i: (i, 0)
    const = lambda i: (0, 0)
    ada_map = lambda i: (i // per, 0, 0)
    last = n // tm - 1
    cur = lambda i: (i,)
    nxt = lambda i: (jnp.minimum(i + 1, last),)
    in_specs = [
        pl.BlockSpec((tm,), cur, memory_space=pltpu.SMEM),
        pl.BlockSpec((tm,), cur, memory_space=pltpu.SMEM),
        pl.BlockSpec((tm,), nxt, memory_space=pltpu.SMEM),
        pl.BlockSpec((tm,), nxt, memory_space=pltpu.SMEM),
        pl.BlockSpec(memory_space=pl.ANY),
        pl.BlockSpec((tm, LANES), row),
        pl.BlockSpec((tm, d), row),
        pl.BlockSpec((1, N_ADA, d), ada_map),
    ]
    args = [p1, p2, p1, p2, ys, rec, xf, ada_l]
    if emit_h:
        in_specs.append(pl.BlockSpec((1, N_ADA, d), ada_map))
        args.append(ada_next)
    in_specs += [pl.BlockSpec((1, d), const), pl.BlockSpec((1, d), const)]
    args += [ln_g.reshape(1, d), ln_b.reshape(1, d)]
    out_specs = [pl.BlockSpec((tm, d), row)]
    out_shape = [jax.ShapeDtypeStruct((n, d), F32)]
    if emit_h:
        out_specs.append(pl.BlockSpec((tm, d), row))
        out_shape.append(jax.ShapeDtypeStruct((n, d), BF16))
    outs = pl.pallas_call(
        functools.partial(_combine_kernel, emit_h=emit_h),
        grid=(n // tm,),
        in_specs=in_specs,
        out_specs=out_specs,
        out_shape=out_shape,
        scratch_shapes=[pltpu.VMEM((2, 2, tm, d), F32), pltpu.SemaphoreType.DMA((2,))],
        compiler_params=_params("arbitrary"),
        name="moe_combine",
    )(*args)
    return outs if emit_h else (outs[0], None)


def _moe_call(hf, rec, counts, wg, wu, wd, expert_base, xf, ada_l, ada_next, ln_g, ln_b, seq, tile, tm):
    n = hf.shape[0]
    total_tiles = 2 * n // tile + N_EXPERTS
    cnt = counts[0, :N_EXPERTS].astype(jnp.int32)
    ntiles = (cnt + tile - 1) // tile
    end_tile = jnp.cumsum(ntiles)
    start_row = (end_tile - ntiles) * tile
    p1 = start_row[rec[:, R_E1].astype(jnp.int32)] + rec[:, R_RANK1].astype(jnp.int32)
    p2 = start_row[rec[:, R_E2].astype(jnp.int32)] + rec[:, R_RANK2].astype(jnp.int32)
    tile_ids = jnp.arange(total_tiles, dtype=jnp.int32)
    tile_expert = jnp.minimum(jnp.sum(tile_ids[:, None] >= end_tile[None, :], axis=1), N_EXPERTS - 1)
    partial_last = jnp.where(cnt % tile != 0, end_tile - 1, -1)
    tail = end_tile[-1] + jnp.arange(N_EXPERTS, dtype=jnp.int32)
    zero_tiles = jnp.concatenate([partial_last, jnp.where(tail < total_tiles, tail, -1)]).astype(jnp.int32)

    xs = _scatter_call(hf, p1, p2, zero_tiles, total_tiles * tile, tile, tm)
    ys = _expert_call(xs, wg, wu, wd, (tile_expert + expert_base).astype(jnp.int32), tile)
    return _combine_call(ys, p1, p2, rec, xf, ada_l, ada_next, ln_g, ln_b, seq, tm)


def kernel(x, c, positions, w_ada, b_ada, w_in, w_dw, b_dw, ln_conv_g, ln_conv_b, w_conv_o, w_ret_o, w_out, ln1_g, ln1_b, ffn_w_gate, ffn_w_up, ffn_w_down, moe_w_router, moe_w_gate, moe_w_up, moe_w_down, ln2_g, ln2_b):
    bsz, seq, d = x.shape
    n = bsz * seq
    tm = 512
    tm_proj = 2048
    tm_mix = 512
    tm_route = 256
    tm_ffn = 256
    tt = 256
    ret_chunk = 256
    moe_tile = 256

    w_in_b = _cast_call(w_in)
    w_conv_o_b = _cast_call(w_conv_o)
    w_ret_o_b = _cast_call(w_ret_o)
    w_out_b = _cast_call(w_out)
    ffn_b = [_cast_call(w) for w in (ffn_w_gate, ffn_w_up, ffn_w_down)]
    moe_b = [_cast_call(w.reshape((-1,) + w.shape[2:])) for w in (moe_w_gate, moe_w_up, moe_w_down)]

    xf = x.reshape(n, d)
    ada = _ada_call(c, w_ada, b_ada)
    cos, sin = _rope_call(positions, tm)
    h = _modulate_call(xf, ada[0], seq, tm)

    for l in range(DEPTH):
        uc = _conv_call(h, w_in_b, l, w_dw[l], b_dw[l], ln_conv_g[l], ln_conv_b[l], bsz, seq, tt)
        qk = _qk_call(h, w_in_b, l, cos, sin, tm_proj)
        vgg = _vgg_call(h, w_in_b, l, tm_proj)
        og = _ret_call(qk, vgg, bsz, seq, ret_chunk)

        i = l // 2
        is_moe = l % 2 == 1
        mixed = _mix_call(
            uc.reshape(n, d), og, vgg, xf, ada[l], w_conv_o_b, w_ret_o_b, w_out_b, l,
            ln1_g[l], ln1_b[l], moe_w_router[i] if is_moe else None, seq,
            tm_route if is_moe else tm_mix)
        ada_next = ada[l + 1] if l + 1 < DEPTH else None
        if is_moe:
            xf, hf, rec, counts = mixed
            xf, h = _moe_call(hf, rec, counts, *moe_b, i * N_EXPERTS, xf, ada[l], ada_next,
                              ln2_g[l], ln2_b[l], seq, moe_tile, tm)
        else:
            xf, h2 = mixed
            xf, h = _ffn_call(h2, *ffn_b, i, xf, ada[l], ada_next, ln2_g[l], ln2_b[l], seq, tm_ffn)
    return xf.reshape(bsz, seq, d)
```

```python
import functools
import math

import jax
import jax.numpy as jnp
from jax import lax
from jax.experimental import pallas as pl
from jax.experimental.pallas import tpu as pltpu

D_MODEL = 1024
DEPTH = 4
CONV_KERNEL = 31
RET_QK_DIM = 256
RET_V_DIM = 512
RET_HEADS = 4
RET_V_WIDTH = RET_HEADS * RET_V_DIM
RET_CHUNK = 128
ROPE_BASE = 10000.0
D_FF = 2816
N_EXPERTS = 8
N_ADA = 6
DEEPNORM_ALPHA = (2 * DEPTH) ** 0.25
LN_EPS = 1e-5

LANES = 128
SUBLANES = 8
CONV_HALO = 32
CONV_ROWS = 128
NORM_ROWS = 32
CAST_ROWS = 512
CAST_MAX_COLS = D_FF
VMEM_LIMIT = 56 * 1024 * 1024

R_E1, R_E2, R_RANK1, R_RANK2, R_W1, R_W2 = range(6)

F32 = jnp.float32
BF16 = jnp.bfloat16


def _params(*sem):
    return pltpu.CompilerParams(dimension_semantics=sem, vmem_limit_bytes=VMEM_LIMIT)


def _bdot(a, b):
    return jnp.dot(a, b, preferred_element_type=F32)


def _split_bf16(a):
    hi = a.astype(BF16)
    lo = (a - hi.astype(F32)).astype(BF16)
    return hi, lo


def _dot3(a, b):
    a_hi, a_lo = _split_bf16(a)
    b_hi, b_lo = _split_bf16(b)
    return _bdot(a_hi, b_hi) + (_bdot(a_hi, b_lo) + _bdot(a_lo, b_hi))


def _sigmoid(x):
    return jax.nn.sigmoid(x)


def _layer_norm_rows(y, g, b):
    mu = jnp.mean(y, axis=-1, keepdims=True)
    d = y - mu
    var = jnp.mean(d * d, axis=-1, keepdims=True)
    return d * lax.rsqrt(var + LN_EPS) * g + b


def _cast_kernel(w_ref, o_ref):
    o_ref[...] = w_ref[...].astype(BF16)


def _cast_call(w):
    cols = w.shape[-1]
    w2 = w.reshape(-1, cols)
    rows = w2.shape[0]
    bc = cols
    while bc > CAST_MAX_COLS:
        bc //= 2
    assert rows % CAST_ROWS == 0 and cols % bc == 0 and bc % LANES == 0, (w.shape, bc)
    out = pl.pallas_call(
        _cast_kernel,
        grid=(rows // CAST_ROWS, cols // bc),
        in_specs=[pl.BlockSpec((CAST_ROWS, bc), lambda i, j: (i, j))],
        out_specs=pl.BlockSpec((CAST_ROWS, bc), lambda i, j: (i, j)),
        out_shape=jax.ShapeDtypeStruct((rows, cols), BF16),
        compiler_params=_params("parallel", "parallel"),
        name="cast_bf16",
    )(w2)
    return out.reshape(w.shape)


def _ada_kernel(c_ref, w_ref, b_ref, o_ref):
    c = c_ref[...]
    o_ref[0] = _dot3(c * _sigmoid(c), w_ref[0]) + b_ref[0]


def _ada_call(c, w_ada, b_ada):
    depth, d, width = w_ada.shape
    bsz = c.shape[0]
    ncol = width // d
    out = pl.pallas_call(
        _ada_kernel,
        grid=(depth, ncol),
        in_specs=[
            pl.BlockSpec((bsz, d), lambda l, j: (0, 0)),
            pl.BlockSpec((1, d, d), lambda l, j: (l, 0, j)),
            pl.BlockSpec((1, 1, d), lambda l, j: (l, 0, j)),
        ],
        out_specs=pl.BlockSpec((1, bsz, d), lambda l, j: (l, 0, j)),
        out_shape=jax.ShapeDtypeStruct((depth, bsz, width), F32),
        compiler_params=_params("parallel", "parallel"),
        name="ada",
    )(c, w_ada, b_ada.reshape(depth, 1, width))
    return out.reshape(depth, bsz, N_ADA, d)


def _rope_kernel(pos_ref, inv_ref, cos_ref, sin_ref):
    ang = pos_ref[...].astype(F32) * inv_ref[...]
    cos_ref[...] = jnp.cos(ang)
    sin_ref[...] = jnp.sin(ang)


def _rope_call(positions, tm):
    n = positions.size
    half = RET_QK_DIM // 2
    inv_freq = ROPE_BASE ** (-jnp.arange(half, dtype=F32) / half)
    return pl.pallas_call(
        _rope_kernel,
        grid=(n // tm,),
        in_specs=[
            pl.BlockSpec((tm, 1), lambda i: (i, 0)),
            pl.BlockSpec((1, half), lambda i: (0, 0)),
        ],
        out_specs=[pl.BlockSpec((tm, half), lambda i: (i, 0))] * 2,
        out_shape=[jax.ShapeDtypeStruct((n, half), F32)] * 2,
        compiler_params=_params("parallel"),
        name="rope",
    )(positions.reshape(n, 1), inv_freq.reshape(1, half))


def _modulate_kernel(x_ref, ada_ref, h_ref):
    shift = ada_ref[0, 0:1, :]
    scale = ada_ref[0, 1:2, :]
    h_ref[...] = (x_ref[...] * (1.0 + scale) + shift).astype(BF16)


def _modulate_call(xf, ada_l, seq, tm):
    n, d = xf.shape
    per = seq // tm
    return pl.pallas_call(
        _modulate_kernel,
        grid=(n // tm,),
        in_specs=[
            pl.BlockSpec((tm, d), lambda i: (i, 0)),
            pl.BlockSpec((1, N_ADA, d), lambda i: (i // per, 0, 0)),
        ],
        out_specs=pl.BlockSpec((tm, d), lambda i: (i, 0)),
        out_shape=jax.ShapeDtypeStruct((n, d), BF16),
        compiler_params=_params("parallel"),
        name="modulate",
    )(xf, ada_l)


def _qk_kernel(h_ref, w_ref, cos_ref, sin_ref, o_ref):
    acc = _bdot(h_ref[...], w_ref[...])
    scale = jnp.where(pl.program_id(0) == 0, 1.0, RET_QK_DIM ** -0.5).astype(F32)
    cos = cos_ref[...] * scale
    sin = sin_ref[...] * scale
    half = RET_QK_DIM // 2
    for hd in range(RET_HEADS):
        lo = hd * RET_QK_DIM
        x1 = acc[:, lo:lo + half]
        x2 = acc[:, lo + half:lo + RET_QK_DIM]
        o_ref[0, :, lo:lo + half] = (x1 * cos - x2 * sin).astype(BF16)
        o_ref[0, :, lo + half:lo + RET_QK_DIM] = (x2 * cos + x1 * sin).astype(BF16)


def _qk_call(h, w_in_b, layer, cos, sin, tm):
    n, d = h.shape
    half = RET_QK_DIM // 2
    return pl.pallas_call(
        _qk_kernel,
        grid=(2, n // tm),
        in_specs=[
            pl.BlockSpec((tm, d), lambda j, i: (i, 0)),
            pl.BlockSpec((None, d, d), lambda j, i: (layer, 0, 2 + j)),
            pl.BlockSpec((tm, half), lambda j, i: (i, 0)),
            pl.BlockSpec((tm, half), lambda j, i: (i, 0)),
        ],
        out_specs=pl.BlockSpec((1, tm, d), lambda j, i: (j, i, 0)),
        out_shape=jax.ShapeDtypeStruct((2, n, d), BF16),
        compiler_params=_params("parallel", "parallel"),
        name="proj_qk",
    )(h, w_in_b, cos, sin)


def _vgg_kernel(h_ref, w_ref, o_ref):
    j = pl.program_id(0)

    @pl.when(j < 2)
    def _():
        o_ref[...] = _bdot(h_ref[...], w_ref[...]).astype(BF16)

    @pl.when(jnp.logical_and(j >= 2, j < 4))
    def _():
        acc = _bdot(h_ref[...], w_ref[...])
        o_ref[...] = (acc * _sigmoid(acc)).astype(BF16)

    @pl.when(j >= 4)
    def _():
        o_ref[...] = _sigmoid(_bdot(h_ref[...], w_ref[...])).astype(BF16)


def _vgg_call(h, w_in_b, layer, tm):
    n, d = h.shape
    col0, ncols = 4, 6
    return pl.pallas_call(
        _vgg_kernel,
        grid=(ncols, n // tm),
        in_specs=[
            pl.BlockSpec((tm, d), lambda j, i: (i, 0)),
            pl.BlockSpec((None, d, d), lambda j, i: (layer, 0, col0 + j)),
        ],
        out_specs=pl.BlockSpec((tm, d), lambda j, i: (i, j)),
        out_shape=jax.ShapeDtypeStruct((n, ncols * d), BF16),
        compiler_params=_params("parallel", "parallel"),
        name="proj_v_gates",
    )(h, w_in_b)


def _conv_kernel(h_ref, wa_ref, wb_ref, w_ref, bias_ref, g_ref, b_ref, o_ref, u_ref, buf_ref, y_ref, *, tt):
    ch = o_ref.shape[-1]
    t = pl.program_id(1)

    @pl.when(t == 0)
    def _():
        u_ref[...] = jnp.zeros_like(u_ref)
        buf_ref[0, tt:, :] = jnp.zeros((CONV_HALO, ch), F32)

    buf_ref[0, 0:CONV_HALO, :] = jnp.where(t <= 1, 0.0, buf_ref[0, tt:, :])
    buf_ref[0, CONV_HALO:, :] = u_ref[...]

    h = h_ref[...]
    u_ref[...] = _bdot(h, wa_ref[...]) * _sigmoid(_bdot(h, wb_ref[...]))

    sub = lax.broadcasted_iota(jnp.int32, (SUBLANES, ch), 0)
    n_shift_tiles = (tt + CONV_HALO) // SUBLANES - 1
    for s in range(1, SUBLANES):
        own = sub < SUBLANES - s
        cur = pltpu.roll(buf_ref[0, 0:SUBLANES, :], SUBLANES - s, 0)
        for j in range(n_shift_tiles):
            nxt = pltpu.roll(buf_ref[0, (j + 1) * SUBLANES:(j + 2) * SUBLANES, :], SUBLANES - s, 0)
            buf_ref[s, j * SUBLANES:(j + 1) * SUBLANES, :] = jnp.where(own, cur, nxt)
            cur = nxt

    base = CONV_HALO - (CONV_KERNEL - 1)
    for lg in range(ch // LANES):
        ls = slice(lg * LANES, (lg + 1) * LANES)
        bias = bias_ref[:, ls]
        for r0 in range(0, tt, CONV_ROWS):
            acc = jnp.zeros((CONV_ROWS, LANES), F32)
            for k in range(CONV_KERNEL):
                a, s = divmod(base + k, SUBLANES)
                lo = r0 + a * SUBLANES
                acc = acc + buf_ref[s, lo:lo + CONV_ROWS, ls] * w_ref[k:k + 1, ls]
            y_ref[r0:r0 + CONV_ROWS, ls] = acc + bias

    g = g_ref[...]
    b = b_ref[...]
    for r0 in range(0, tt, NORM_ROWS):
        y = _layer_norm_rows(y_ref[r0:r0 + NORM_ROWS, :], g, b)
        o_ref[0, r0:r0 + NORM_ROWS, :] = (y * _sigmoid(y)).astype(BF16)


def _conv_call(h, w_in_b, layer, w_dw, b_dw, ln_g, ln_b, bsz, seq, tt):
    n, d = h.shape
    ch = w_dw.shape[-1]
    nt = seq // tt
    const = lambda b, t: (0, 0)
    return pl.pallas_call(
        functools.partial(_conv_kernel, tt=tt),
        grid=(bsz, nt + 1),
        in_specs=[
            pl.BlockSpec((tt, d), lambda b, t: (b * nt + jnp.minimum(t, nt - 1), 0)),
            pl.BlockSpec((None, d, ch), lambda b, t: (layer, 0, 0)),
            pl.BlockSpec((None, d, ch), lambda b, t: (layer, 0, 1)),
            pl.BlockSpec((CONV_KERNEL, ch), const),
            pl.BlockSpec((1, ch), const),
            pl.BlockSpec((1, ch), const),
            pl.BlockSpec((1, ch), const),
        ],
        out_specs=pl.BlockSpec((1, tt, ch), lambda b, t: (b, jnp.maximum(t - 1, 0), 0)),
        out_shape=jax.ShapeDtypeStruct((bsz, seq, ch), BF16),
        scratch_shapes=[pltpu.VMEM((tt, ch), F32),
                        pltpu.VMEM((SUBLANES, CONV_HALO + tt, ch), F32),
                        pltpu.VMEM((tt, ch), F32)],
        compiler_params=_params("parallel", "arbitrary"),
        name="conv_glu_ln",
    )(h, w_in_b, w_in_b, w_dw, b_dw.reshape(1, ch), ln_g.reshape(1, ch), ln_b.reshape(1, ch))


def _ret_kernel(q_ref, k_ref, v_ref, g_ref, o_ref, s_ref, intra_ref, *, chunk):
    log_gamma = [math.log1p(-(2.0 ** (-5 - hd))) for hd in range(RET_HEADS)]

    @pl.when(pl.program_id(1) == 0)
    def _():
        s_ref[...] = jnp.zeros_like(s_ref)
        diff = (lax.broadcasted_iota(jnp.int32, (chunk, chunk), 0)
                - lax.broadcasted_iota(jnp.int32, (chunk, chunk), 1)).astype(F32)
        for hd in range(RET_HEADS):
            intra_ref[hd] = jnp.where(diff >= 0, jnp.exp(log_gamma[hd] * jnp.maximum(diff, 0.0)), 0.0)

    row = lax.broadcasted_iota(jnp.int32, (chunk, 1), 0).astype(F32)
    for hd in range(RET_HEADS):
        log_g = log_gamma[hd]
        q_decay = jnp.exp(log_g * (row + 1.0))
        k_decay = jnp.exp(log_g * (chunk - 1.0 - row))
        chunk_decay = math.exp(log_g * chunk)
        intra = intra_ref[hd]

        qs = slice(hd * RET_QK_DIM, (hd + 1) * RET_QK_DIM)
        vs = slice(hd * RET_V_DIM, (hd + 1) * RET_V_DIM)
        q = q_ref[:, qs]
        k = k_ref[:, qs]
        v = v_ref[:, vs]
        state = s_ref[hd]
        scores = lax.dot_general(q, k, (((1,), (1,)), ((), ())), preferred_element_type=F32)
        inner = _bdot((scores * intra).astype(BF16), v)
        cross = _bdot(q, state.astype(BF16)) * q_decay
        update = lax.dot_general((k.astype(F32) * k_decay).astype(BF16), v, (((0,), (0,)), ((), ())),
                                 preferred_element_type=F32)
        s_ref[hd] = state * chunk_decay + update

        o = inner + cross
        mu = jnp.mean(o, axis=-1, keepdims=True)
        d = o - mu
        var = jnp.mean(d * d, axis=-1, keepdims=True)
        o_ref[:, vs] = (g_ref[:, vs] * (d * lax.rsqrt(var + LN_EPS))).astype(BF16)


def _ret_call(qk, vgg, bsz, seq, chunk):
    width = RET_HEADS * RET_QK_DIM
    qk4 = qk.reshape(2, bsz, seq, width)
    v3 = g3 = vgg.reshape(bsz, seq, vgg.shape[-1])
    out = pl.pallas_call(
        functools.partial(_ret_kernel, chunk=chunk),
        grid=(bsz, seq // chunk),
        in_specs=[
            pl.BlockSpec((None, None, chunk, width), lambda b, n: (0, b, n, 0)),
            pl.BlockSpec((None, None, chunk, width), lambda b, n: (1, b, n, 0)),
            pl.BlockSpec((None, chunk, RET_V_WIDTH), lambda b, n: (b, n, 0)),
            pl.BlockSpec((None, chunk, RET_V_WIDTH), lambda b, n: (b, n, 1)),
        ],
        out_specs=pl.BlockSpec((None, chunk, RET_V_WIDTH), lambda b, n: (b, n, 0)),
        out_shape=jax.ShapeDtypeStruct((bsz, seq, RET_V_WIDTH), BF16),
        scratch_shapes=[pltpu.VMEM((RET_HEADS, RET_QK_DIM, RET_V_DIM), F32),
                        pltpu.VMEM((RET_HEADS, chunk, chunk), F32)],
        compiler_params=_params("parallel", "arbitrary"),
        name="retention",
    )(qk4, qk4, v3, g3)
    return out.reshape(bsz * seq, RET_V_WIDTH)


def _route(logits, carry_ref):
    rows = logits.shape[0]
    lane = lax.broadcasted_iota(jnp.int32, logits.shape, 1)
    neg = jnp.float32(-jnp.inf)
    lg = jnp.where(lane < N_EXPERTS, logits, neg)
    m1 = jnp.max(lg, axis=-1, keepdims=True)
    i1 = jnp.min(jnp.where(lg == m1, lane, LANES), axis=-1, keepdims=True)
    lg2 = jnp.where(lane == i1, neg, lg)
    m2 = jnp.max(lg2, axis=-1, keepdims=True)
    i2 = jnp.min(jnp.where(lg2 == m2, lane, LANES), axis=-1, keepdims=True)
    e = jnp.exp(m2 - m1)
    w1 = 1.0 / (1.0 + e)
    w2 = e / (1.0 + e)

    sel = jnp.where(jnp.logical_or(lane == i1, lane == i2), 1.0, 0.0)
    tri = jnp.where(lax.broadcasted_iota(jnp.int32, (rows, rows), 0)
                    >= lax.broadcasted_iota(jnp.int32, (rows, rows), 1), 1.0, 0.0).astype(BF16)
    incl = _bdot(tri, sel.astype(BF16))
    rank = carry_ref[...] + incl - sel
    carry_ref[...] = carry_ref[...] + incl[rows - 1:rows, :]
    r1 = jnp.sum(jnp.where(lane == i1, rank, 0.0), axis=-1, keepdims=True)
    r2 = jnp.sum(jnp.where(lane == i2, rank, 0.0), axis=-1, keepdims=True)

    rec = jnp.zeros(logits.shape, F32)
    for idx, val in ((R_E1, i1.astype(F32)), (R_E2, i2.astype(F32)), (R_RANK1, r1), (R_RANK2, r2),
                     (R_W1, w1), (R_W2, w2)):
        rec = jnp.where(lane == idx, val, rec)
    return rec


def _mix_kernel(*refs, router):
    if router:
        (uc_ref, og_ref, gt_ref, x_ref, ada_ref, wc_ref, wr_ref, wo_ref, g_ref, b_ref, wrt_ref,
         xo_ref, ho_ref, rec_ref, cnt_ref, carry_ref) = refs
    else:
        (uc_ref, og_ref, gt_ref, x_ref, ada_ref, wc_ref, wr_ref, wo_ref, g_ref, b_ref,
         xo_ref, ho_ref) = refs
    d = x_ref.shape[-1]
    y_conv = _bdot(uc_ref[...], wc_ref[...])
    y_ret = _bdot(og_ref[...], wr_ref[...])
    merged = gt_ref[:, 0:d] * y_conv + gt_ref[:, d:2 * d] * y_ret
    mix = _bdot(merged.astype(BF16), wo_ref[...])
    gate1 = ada_ref[0, 2:3, :]
    shift2 = ada_ref[0, 3:4, :]
    scale2 = ada_ref[0, 4:5, :]
    xn = _layer_norm_rows(DEEPNORM_ALPHA * x_ref[...] + gate1 * mix, g_ref[...], b_ref[...])
    xo_ref[...] = xn
    h2 = xn * (1.0 + scale2) + shift2
    ho_ref[...] = h2.astype(ho_ref.dtype)
    if router:
        @pl.when(pl.program_id(0) == 0)
        def _():
            carry_ref[...] = jnp.zeros_like(carry_ref)

        rec_ref[...] = _route(_dot3(h2, wrt_ref[...]), carry_ref)
        cnt_ref[...] = carry_ref[...]


def _mix_call(uc, og, gates, xf, ada_l, wc, wr, wo, layer, ln_g, ln_b, w_router, seq, tm):
    n, d = xf.shape
    per = seq // tm
    router = w_router is not None
    row = lambda i: (i, 0)
    const = lambda i: (0, 0)
    in_specs = [
        pl.BlockSpec((tm, d), row),
        pl.BlockSpec((tm, RET_V_WIDTH), row),
        pl.BlockSpec((tm, 2 * d), lambda i: (i, 2)),
        pl.BlockSpec((tm, d), row),
        pl.BlockSpec((1, N_ADA, d), lambda i: (i // per, 0, 0)),
        pl.BlockSpec((None, d, d), lambda i: (layer, 0, 0), pipeline_mode=pl.Buffered(1)),
        pl.BlockSpec((None, RET_V_WIDTH, d), lambda i: (layer, 0, 0), pipeline_mode=pl.Buffered(1)),
        pl.BlockSpec((None, d, d), lambda i: (layer, 0, 0), pipeline_mode=pl.Buffered(1)),
        pl.BlockSpec((1, d), const),
        pl.BlockSpec((1, d), const),
    ]
    args = [uc, og, gates, xf, ada_l, wc, wr, wo, ln_g.reshape(1, d), ln_b.reshape(1, d)]
    out_specs = [pl.BlockSpec((tm, d), row), pl.BlockSpec((tm, d), row)]
    out_shape = [jax.ShapeDtypeStruct((n, d), F32), jax.ShapeDtypeStruct((n, d), F32 if router else BF16)]
    scratch = []
    if router:
        wrt = jnp.zeros((d, LANES), F32).at[:, :N_EXPERTS].set(w_router)
        in_specs.append(pl.BlockSpec((d, LANES), const))
        args.append(wrt)
        out_specs += [pl.BlockSpec((tm, LANES), row), pl.BlockSpec((1, LANES), const)]
        out_shape += [jax.ShapeDtypeStruct((n, LANES), F32), jax.ShapeDtypeStruct((1, LANES), F32)]
        scratch = [pltpu.VMEM((1, LANES), F32)]
    return pl.pallas_call(
        functools.partial(_mix_kernel, router=router),
        grid=(n // tm,),
        in_specs=in_specs,
        out_specs=out_specs,
        out_shape=out_shape,
        scratch_shapes=scratch,
        compiler_params=_params("arbitrary" if router else "parallel"),
        name="mix_out_router" if router else "mix_out",
    )(*args)


def _residual_epilogue(ff, x_ref, ada_ref, adan_ref, g_ref, b_ref, xo_ref, ho_ref):
    gate2 = ada_ref[0, 5:6, :]
    xn = _layer_norm_rows(DEEPNORM_ALPHA * x_ref[...] + gate2 * ff, g_ref[...], b_ref[...])
    xo_ref[...] = xn
    if ho_ref is not None:
        shift1 = adan_ref[0, 0:1, :]
        scale1 = adan_ref[0, 1:2, :]
        ho_ref[...] = (xn * (1.0 + scale1) + shift1).astype(BF16)


def _swiglu_cols(h, wg_ref, wu_ref, wd_ref, lo, hi):
    gate = _bdot(h, wg_ref[0, :, lo:hi])
    up = _bdot(h, wu_ref[0, :, lo:hi])
    return _bdot((gate * _sigmoid(gate) * up).astype(BF16), wd_ref[0, lo:hi, :])


def _ffn_kernel(*refs, emit_h):
    if emit_h:
        h_ref, wg_ref, wu_ref, wd_ref, x_ref, ada_ref, adan_ref, g_ref, b_ref, xo_ref, ho_ref = refs
    else:
        h_ref, wg_ref, wu_ref, wd_ref, x_ref, ada_ref, g_ref, b_ref, xo_ref = refs
        adan_ref = ho_ref = None
    h = h_ref[...]
    half = wg_ref.shape[-1] // 2
    ff = (_swiglu_cols(h, wg_ref, wu_ref, wd_ref, 0, half)
          + _swiglu_cols(h, wg_ref, wu_ref, wd_ref, half, 2 * half))
    _residual_epilogue(ff, x_ref, ada_ref, adan_ref, g_ref, b_ref, xo_ref, ho_ref)


def _ffn_call(h, wg, wu, wd, layer, xf, ada_l, ada_next, ln_g, ln_b, seq, tm):
    n, d = xf.shape
    dff = wg.shape[-1]
    per = seq // tm
    emit_h = ada_next is not None
    row = lambda i: (i, 0)
    const = lambda i: (0, 0)
    ada_map = lambda i: (i // per, 0, 0)
    in_specs = [
        pl.BlockSpec((tm, d), row),
        pl.BlockSpec((1, d, dff), lambda i: (layer, 0, 0)),
        pl.BlockSpec((1, d, dff), lambda i: (layer, 0, 0)),
        pl.BlockSpec((1, dff, d), lambda i: (layer, 0, 0)),
        pl.BlockSpec((tm, d), row),
        pl.BlockSpec((1, N_ADA, d), ada_map),
    ]
    args = [h, wg, wu, wd, xf, ada_l]
    if emit_h:
        in_specs.append(pl.BlockSpec((1, N_ADA, d), ada_map))
        args.append(ada_next)
    in_specs += [pl.BlockSpec((1, d), const), pl.BlockSpec((1, d), const)]
    args += [ln_g.reshape(1, d), ln_b.reshape(1, d)]
    out_specs = [pl.BlockSpec((tm, d), row)]
    out_shape = [jax.ShapeDtypeStruct((n, d), F32)]
    if emit_h:
        out_specs.append(pl.BlockSpec((tm, d), row))
        out_shape.append(jax.ShapeDtypeStruct((n, d), BF16))
    outs = pl.pallas_call(
        functools.partial(_ffn_kernel, emit_h=emit_h),
        grid=(n // tm,),
        in_specs=in_specs,
        out_specs=out_specs,
        out_shape=out_shape,
        compiler_params=_params("parallel"),
        name="dense_ffn",
    )(*args)
    return outs if emit_h else (outs[0], None)


def _row_copy(src, src_row, dst, dst_row, sem):
    return pltpu.make_async_copy(src.at[pl.ds(src_row, 1)], dst.at[pl.ds(dst_row, 1)], sem)


def _scatter_kernel(zt_ref, p1_ref, p2_ref, hf_ref, xs_hbm, zero_ref, sem):
    i = pl.program_id(0)
    tb = p1_ref.shape[0]
    tile = zero_ref.shape[0]

    def zero_copy(j):
        return pltpu.make_async_copy(zero_ref, xs_hbm.at[pl.ds(zt_ref[j] * tile, tile)], sem)

    @pl.when(i == 0)
    def _():
        zero_ref[...] = jnp.zeros_like(zero_ref)
        for j in range(zt_ref.shape[0]):
            pl.when(zt_ref[j] >= 0)(lambda j=j: zero_copy(j).start())
        for j in range(zt_ref.shape[0]):
            pl.when(zt_ref[j] >= 0)(lambda j=j: zero_copy(j).wait())

    def start(k, carry):
        _row_copy(hf_ref, k, xs_hbm, p1_ref[k], sem).start(priority=0)
        _row_copy(hf_ref, k, xs_hbm, p2_ref[k], sem).start(priority=1)
        return carry

    lax.fori_loop(0, tb, start, 0, unroll=8)
    for _ in range(2):
        pltpu.make_async_copy(hf_ref, xs_hbm.at[pl.ds(0, tb)], sem).wait()


def _scatter_call(hf, p1, p2, zero_tiles, rows, tile, tb):
    n, d = hf.shape
    return pl.pallas_call(
        _scatter_kernel,
        grid_spec=pltpu.PrefetchScalarGridSpec(
            num_scalar_prefetch=1,
            grid=(n // tb,),
            in_specs=[
                pl.BlockSpec((tb,), lambda i, zt: (i,), memory_space=pltpu.SMEM),
                pl.BlockSpec((tb,), lambda i, zt: (i,), memory_space=pltpu.SMEM),
                pl.BlockSpec((tb, d), lambda i, zt: (i, 0)),
            ],
            out_specs=pl.BlockSpec(memory_space=pl.ANY),
            scratch_shapes=[pltpu.VMEM((tile, d), F32), pltpu.SemaphoreType.DMA(())],
        ),
        out_shape=jax.ShapeDtypeStruct((rows, d), F32),
        compiler_params=_params("arbitrary"),
        name="moe_scatter",
    )(zero_tiles, p1, p2, hf)


def _expert_kernel(te_ref, xs_ref, wg_ref, wu_ref, wd_ref, ys_ref):
    h = xs_ref[...].astype(BF16)
    half = wg_ref.shape[-1] // 2
    ys_ref[...] = (_swiglu_cols(h, wg_ref, wu_ref, wd_ref, 0, half)
                   + _swiglu_cols(h, wg_ref, wu_ref, wd_ref, half, 2 * half))


def _expert_call(xs, wg, wu, wd, tile_expert, tile):
    rows, d = xs.shape
    dff = wg.shape[-1]
    return pl.pallas_call(
        _expert_kernel,
        grid_spec=pltpu.PrefetchScalarGridSpec(
            num_scalar_prefetch=1,
            grid=(rows // tile,),
            in_specs=[
                pl.BlockSpec((tile, d), lambda r, te: (r, 0)),
                pl.BlockSpec((1, d, dff), lambda r, te: (te[r], 0, 0)),
                pl.BlockSpec((1, d, dff), lambda r, te: (te[r], 0, 0)),
                pl.BlockSpec((1, dff, d), lambda r, te: (te[r], 0, 0)),
            ],
            out_specs=pl.BlockSpec((tile, d), lambda r, te: (r, 0)),
        ),
        out_shape=jax.ShapeDtypeStruct((rows, d), F32),
        compiler_params=_params("arbitrary"),
        name="moe_experts",
    )(tile_expert, xs, wg, wu, wd)


def _combine_kernel(*refs, emit_h):
    if emit_h:
        (p1_ref, p2_ref, p1n_ref, p2n_ref, ys_hbm, rec_ref, x_ref, ada_ref, adan_ref, g_ref, b_ref,
         xo_ref, ho_ref, buf_ref, sem) = refs
    else:
        (p1_ref, p2_ref, p1n_ref, p2n_ref, ys_hbm, rec_ref, x_ref, ada_ref, g_ref, b_ref,
         xo_ref, buf_ref, sem) = refs
        adan_ref = ho_ref = None
    tm = x_ref.shape[0]
    i = pl.program_id(0)
    slot = i % 2

    def rows(pa_ref, pb_ref, s, k):
        return (_row_copy(ys_hbm, pa_ref[k], buf_ref.at[s, 0], k, sem.at[s]),
                _row_copy(ys_hbm, pb_ref[k], buf_ref.at[s, 1], k, sem.at[s]))

    def start_tile(pa_ref, pb_ref, s):
        def body(k, carry):
            a, b = rows(pa_ref, pb_ref, s, k)
            a.start(priority=0)
            b.start(priority=1)
            return carry
        lax.fori_loop(0, tm, body, 0, unroll=8)

    @pl.when(i == 0)
    def _():
        start_tile(p1_ref, p2_ref, 0)

    @pl.when(i + 1 < pl.num_programs(0))
    def _():
        start_tile(p1n_ref, p2n_ref, 1 - slot)

    for half in range(2):
        pltpu.make_async_copy(ys_hbm.at[pl.ds(0, tm)], buf_ref.at[slot, half], sem.at[slot]).wait()
    w1 = rec_ref[:, R_W1:R_W1 + 1]
    w2 = rec_ref[:, R_W2:R_W2 + 1]
    ff = w1 * buf_ref[slot, 0] + w2 * buf_ref[slot, 1]
    _residual_epilogue(ff, x_ref, ada_ref, adan_ref, g_ref, b_ref, xo_ref, ho_ref)


def _combine_call(ys, p1, p2, rec, xf, ada_l, ada_next, ln_g, ln_b, seq, tm):
    n, d = xf.shape
    per = seq // tm
    emit_h = ada_next is not None
    row = lambda i: (i, 0)
    const = lambda i: (0, 0)
    ada_map = lambda i: (i // per, 0, 0)
    last = n // tm - 1
    cur = lambda i: (i,)
    nxt = lambda i: (jnp.minimum(i + 1, last),)
    in_specs = [
        pl.BlockSpec((tm,), cur, memory_space=pltpu.SMEM),
        pl.BlockSpec((tm,), cur, memory_space=pltpu.SMEM),
        pl.BlockSpec((tm,), nxt, memory_space=pltpu.SMEM),
        pl.BlockSpec((tm,), nxt, memory_space=pltpu.SMEM),
        pl.BlockSpec(memory_space=pl.ANY),
        pl.BlockSpec((tm, LANES), row),
        pl.BlockSpec((tm, d), row),
        pl.BlockSpec((1, N_ADA, d), ada_map),
    ]
    args = [p1, p2, p1, p2, ys, rec, xf, ada_l]
    if emit_h:
        in_specs.append(pl.BlockSpec((1, N_ADA, d), ada_map))
        args.append(ada_next)
    in_specs += [pl.BlockSpec((1, d), const), pl.BlockSpec((1, d), const)]
    args += [ln_g.reshape(1, d), ln_b.reshape(1, d)]
    out_specs = [pl.BlockSpec((tm, d), row)]
    out_shape = [jax.ShapeDtypeStruct((n, d), F32)]
    if emit_h:
        out_specs.append(pl.BlockSpec((tm, d), row))
        out_shape.append(jax.ShapeDtypeStruct((n, d), BF16))
    outs = pl.pallas_call(
        functools.partial(_combine_kernel, emit_h=emit_h),
        grid=(n // tm,),
        in_specs=in_specs,
        out_specs=out_specs,
        out_shape=out_shape,
        scratch_shapes=[pltpu.VMEM((2, 2, tm, d), F32), pltpu.SemaphoreType.DMA((2,))],
        compiler_params=_params("arbitrary"),
        name="moe_combine",
    )(*args)
    return outs if emit_h else (outs[0], None)


def _moe_call(hf, rec, counts, wg, wu, wd, expert_base, xf, ada_l, ada_next, ln_g, ln_b, seq, tile, tm):
    n = hf.shape[0]
    total_tiles = 2 * n // tile + N_EXPERTS
    cnt = counts[0, :N_EXPERTS].astype(jnp.int32)
    ntiles = (cnt + tile - 1) // tile
    end_tile = jnp.cumsum(ntiles)
    start_row = (end_tile - ntiles) * tile
    p1 = start_row[rec[:, R_E1].astype(jnp.int32)] + rec[:, R_RANK1].astype(jnp.int32)
    p2 = start_row[rec[:, R_E2].astype(jnp.int32)] + rec[:, R_RANK2].astype(jnp.int32)
    tile_ids = jnp.arange(total_tiles, dtype=jnp.int32)
    tile_expert = jnp.minimum(jnp.sum(tile_ids[:, None] >= end_tile[None, :], axis=1), N_EXPERTS - 1)
    partial_last = jnp.where(cnt % tile != 0, end_tile - 1, -1)
    tail = end_tile[-1] + jnp.arange(N_EXPERTS, dtype=jnp.int32)
    zero_tiles = jnp.concatenate([partial_last, jnp.where(tail < total_tiles, tail, -1)]).astype(jnp.int32)

    xs = _scatter_call(hf, p1, p2, zero_tiles, total_tiles * tile, tile, tm)
    ys = _expert_call(xs, wg, wu, wd, (tile_expert + expert_base).astype(jnp.int32), tile)
    return _combine_call(ys, p1, p2, rec, xf, ada_l, ada_next, ln_g, ln_b, seq, tm)


def kernel(x, c, positions, w_ada, b_ada, w_in, w_dw, b_dw, ln_conv_g, ln_conv_b, w_conv_o, w_ret_o, w_out, ln1_g, ln1_b, ffn_w_gate, ffn_w_up, ffn_w_down, moe_w_router, moe_w_gate, moe_w_up, moe_w_down, ln2_g, ln2_b):
    bsz, seq, d = x.shape
    n = bsz * seq
    tm = 512
    tm_proj = 2048
    tm_mix = 512
    tm_route = 256
    tm_ffn = 256
    tt = 256
    ret_chunk = 256
    moe_tile = 256

    w_in_b = _cast_call(w_in)
    w_conv_o_b = _cast_call(w_conv_o)
    w_ret_o_b = _cast_call(w_ret_o)
    w_out_b = _cast_call(w_out)
    ffn_b = [_cast_call(w) for w in (ffn_w_gate, ffn_w_up, ffn_w_down)]
    moe_b = [_cast_call(w.reshape((-1,) + w.shape[2:])) for w in (moe_w_gate, moe_w_up, moe_w_down)]

    xf = x.reshape(n, d)
    ada = _ada_call(c, w_ada, b_ada)
    cos, sin = _rope_call(positions, tm)
    h = _modulate_call(xf, ada[0], seq, tm)

    for l in range(DEPTH):
        uc = _conv_call(h, w_in_b, l, w_dw[l], b_dw[l], ln_conv_g[l], ln_conv_b[l], bsz, seq, tt)
        qk = _qk_call(h, w_in_b, l, cos, sin, tm_proj)
        vgg = _vgg_call(h, w_in_b, l, tm_proj)
        og = _ret_call(qk, vgg, bsz, seq, ret_chunk)

        i = l // 2
        is_moe = l % 2 == 1
        mixed = _mix_call(
            uc.reshape(n, d), og, vgg, xf, ada[l], w_conv_o_b, w_ret_o_b, w_out_b, l,
            ln1_g[l], ln1_b[l], moe_w_router[i] if is_moe else None, seq,
            tm_route if is_moe else tm_mix)
        ada_next = ada[l + 1] if l + 1 < DEPTH else None
        if is_moe:
            xf, hf, rec, counts = mixed
            xf, h = _moe_call(hf, rec, counts, *moe_b, i * N_EXPERTS, xf, ada[l], ada_next,
                              ln2_g[l], ln2_b[l], seq, moe_tile, tm)
        else:
            xf, h2 = mixed
            xf, h = _ffn_call(h2, *ffn_b, i, xf, ada[l], ada_next, ln2_g[l], ln2_b[l], seq, tm_ffn)
    return xf.reshape(bsz, seq, d)
```
